```python
import math
import jax
import jax.numpy as jnp
from jax import lax
import numpy as np

D_MODEL = 1024
BATCH = 16
SEQ = 2048
DEPTH = 1

BLOCK = 128
EPS = 1e-6

MLA_HEADS = 8
MLA_NOPE = 64
MLA_ROPE = 32
MLA_V = 64
MLA_KV_RANK = 256
ROPE_THETA = 10000.0

DIL_PATTERNS = ((128, 1), (512, 4), (2048, 16))
N_DIL_GROUPS = 3
DIL_HEADS = 8
DIL_HEAD_DIM = 64

N_EXPERT_GROUPS = 4
EXPERTS_PER_GROUP = 8
N_EXPERTS = N_EXPERT_GROUPS * EXPERTS_PER_GROUP
TOP_K_EXPERT = 2
D_FF_EXPERT = 256

MLA_Q_COLS = MLA_HEADS * (MLA_NOPE + MLA_ROPE)
DIL_COLS = 3 * N_DIL_GROUPS * DIL_HEADS * DIL_HEAD_DIM
GATE_COLS = 2 * D_MODEL
IN_COLS = MLA_Q_COLS + MLA_KV_RANK + MLA_ROPE + DIL_COLS + GATE_COLS

kernel_name = 'hybrid_mla_dilated_gated_hmoe'


def rms_norm(x, g):
    xf = x.astype(jnp.float32)
    y = xf * lax.rsqrt(jnp.mean(xf * xf, axis=-1, keepdims=True) + EPS)
    return (y * g.astype(jnp.float32)).astype(x.dtype)


def alibi_slopes(n):
    return jnp.exp2(-8.0 * (jnp.arange(n, dtype=jnp.float32) + 1.0) / n)


def rope(t, positions):
    half = t.shape[-1] // 2
    freqs = ROPE_THETA ** (-jnp.arange(half, dtype=jnp.float32) / half)
    ang = positions.astype(jnp.float32)[:, :, None] * freqs
    cos = jnp.cos(ang)[:, :, None, :]
    sin = jnp.sin(ang)[:, :, None, :]
    t1 = t[..., :half].astype(jnp.float32)
    t2 = t[..., half:].astype(jnp.float32)
    return jnp.concatenate([t1 * cos - t2 * sin, t2 * cos + t1 * sin], axis=-1).astype(t.dtype)


def causal_block_attention(q, k, v, scale):
    B, S, H, dq = q.shape
    nb = S // BLOCK
    q_blocks = jnp.moveaxis(q.reshape(B, nb, BLOCK, H, dq), 1, 0)
    kpos = jnp.arange(S)

    def one_block(args):
        qb, bi = args
        s = jnp.einsum('bqhd,bkhd->bhqk', qb, k).astype(jnp.float32) * scale
        qpos = bi * BLOCK + jnp.arange(BLOCK)
        s = jnp.where(kpos[None, :] <= qpos[:, None], s, -jnp.inf)
        p = jax.nn.softmax(s, axis=-1)
        return jnp.einsum('bhqk,bkhd->bqhd', p.astype(v.dtype), v)

    o = lax.map(one_block, (q_blocks, jnp.arange(nb)))
    return jnp.moveaxis(o, 0, 1).reshape(B, S, H, v.shape[-1])


def banded_attention(q, k, v, n_back, stride, slopes, scale):
    assert n_back <= BLOCK
    N, L, H, dh = q.shape
    nb = -(-L // BLOCK)
    Lp = nb * BLOCK
    qb = jnp.pad(q, ((0, 0), (0, Lp - L), (0, 0), (0, 0))).reshape(N, nb, BLOCK, H, dh)

    def key_blocks(t):
        tp = jnp.pad(t, ((0, 0), (BLOCK, Lp - L), (0, 0), (0, 0)))
        prev = tp[:, :Lp].reshape(N, nb, BLOCK, H, dh)
        cur = tp[:, BLOCK:].reshape(N, nb, BLOCK, H, dh)
        return jnp.concatenate([prev, cur], axis=2)

    kb = key_blocks(k)
    vb = key_blocks(v)
    s = jnp.einsum('nbqhd,nbkhd->nbhqk', qb, kb).astype(jnp.float32) * scale
    i = jnp.arange(BLOCK)[:, None]
    j = jnp.arange(2 * BLOCK)[None, :]
    delta = i + BLOCK - j
    kpos = jnp.arange(nb)[:, None, None] * BLOCK - BLOCK + j[None]
    valid = (delta >= 0) & (delta <= n_back) & (kpos >= 0)
    bias = -slopes[:, None, None] * (delta * stride).astype(jnp.float32)
    s = jnp.where(valid[None, :, None], s + bias, -jnp.inf)
    m = jnp.max(s, axis=-1, keepdims=True)
    p = jnp.exp(s - m)
    den = jnp.sum(p, axis=-1, keepdims=True)
    o = jnp.einsum('nbhqk,nbkhd->nbqhd', (p / den).astype(v.dtype), vb)
    o = o.reshape(N, Lp, H, dh)[:, :L]
    lse = jnp.transpose((m + jnp.log(den))[..., 0], (0, 1, 3, 2)).reshape(N, Lp, H)[:, :L]
    return o, lse


def mla_branch(q_cols, ckv, k_rope, positions, norm_ckv, w_ukv, q_norm, k_norm):
    B, S, _ = q_cols.shape
    q = q_cols.reshape(B, S, MLA_HEADS, MLA_NOPE + MLA_ROPE)
    kv = (rms_norm(ckv, norm_ckv) @ w_ukv).reshape(B, S, MLA_HEADS, MLA_NOPE + MLA_V)
    k_nope, v = kv[..., :MLA_NOPE], kv[..., MLA_NOPE:]
    k = jnp.concatenate([k_nope, jnp.broadcast_to(k_rope[:, :, None, :], (B, S, MLA_HEADS, MLA_ROPE))], axis=-1)
    q = rms_norm(q, q_norm)
    k = rms_norm(k, k_norm)
    q = jnp.concatenate([q[..., :MLA_NOPE], rope(q[..., MLA_NOPE:], positions)], axis=-1)
    k = jnp.concatenate([k[..., :MLA_NOPE], rope(k[..., MLA_NOPE:], positions)], axis=-1)
    o = causal_block_attention(q, k, v, (MLA_NOPE + MLA_ROPE) ** -0.5)
    return o.reshape(B, S, MLA_HEADS * MLA_V)


def dilated_branch(qkv_cols, q_norm, k_norm):
    B, S, _ = qkv_cols.shape
    H, dh = DIL_HEADS, DIL_HEAD_DIM
    qkv = qkv_cols.reshape(B, S, 3, N_DIL_GROUPS, H, dh)
    q = rms_norm(qkv[:, :, 0], q_norm[:, None, :])
    k = rms_norm(qkv[:, :, 1], k_norm[:, None, :])
    v = qkv[:, :, 2]
    slopes = alibi_slopes(H)
    outs, lses = [], []
    for g, (window, dil) in enumerate(DIL_PATTERNS):
        L = S // dil
        def fold(t):
            return t[:, :, g].reshape(B, L, dil, H, dh).transpose(0, 2, 1, 3, 4).reshape(B * dil, L, H, dh)
        o, lse = banded_attention(fold(q), fold(k), fold(v), window // dil, dil, slopes, dh ** -0.5)
        outs.append(o.reshape(B, dil, L, H, dh).transpose(0, 2, 1, 3, 4).reshape(B, S, H, dh))
        lses.append(lse.reshape(B, dil, L, H).transpose(0, 2, 1, 3).reshape(B, S, H))
    w = jax.nn.softmax(jnp.stack(lses, axis=0), axis=0)
    o = jnp.sum(w[..., None] * jnp.stack(outs, axis=0).astype(jnp.float32), axis=0)
    return o.astype(qkv_cols.dtype).reshape(B, S, H * dh)


def hier_moe(h, w_rg, b_rg, w_re, b_re, w_gate, w_up, w_down):
    B, S, D = h.shape
    T = B * S
    hf = h.reshape(T, D)
    gp = jax.nn.softmax((hf @ w_rg + b_rg).astype(jnp.float32), axis=-1)
    g_p, g_i = lax.top_k(gp, 1)
    g_p, g_i = g_p[:, 0], g_i[:, 0]
    el = (hf @ w_re + b_re).astype(jnp.float32).reshape(T, N_EXPERT_GROUPS, EXPERTS_PER_GROUP)
    el_sel = el[jnp.arange(T), g_i]
    ep = jax.nn.softmax(el_sel, axis=-1)
    e_p, e_i = lax.top_k(ep, TOP_K_EXPERT)
    e_p = e_p / jnp.sum(e_p, axis=-1, keepdims=True)
    weights = g_p[:, None] * e_p
    expert_idx = g_i[:, None] * EXPERTS_PER_GROUP + e_i
    combine = jnp.sum(jax.nn.one_hot(expert_idx, N_EXPERTS, dtype=jnp.float32) * weights[..., None], axis=1)
    combine = combine.astype(hf.dtype)
    y = jnp.zeros_like(hf)
    for e in range(N_EXPERTS):
        he = jax.nn.silu(hf @ w_gate[e]) * (hf @ w_up[e])
        y = y + combine[:, e:e + 1] * (he @ w_down[e])
    return y.reshape(B, S, D)


def setup_inputs(seed: int = 0) -> dict:
    key = jax.random.key(seed)
    ks = jax.random.split(key, 24)

    def nrm(k, shape, fan_in):
        return jax.random.normal(k, shape, jnp.float32) * (fan_in ** -0.5)

    def gain(k, shape):
        return 1.0 + 0.02 * jax.random.normal(k, shape, jnp.float32)

    def bias(k, shape, scale):
        return scale * jax.random.normal(k, shape, jnp.float32)

    Ld = DEPTH
    return {
        'x': jax.random.normal(ks[0], (BATCH, SEQ, D_MODEL), jnp.float32),
        'positions': jnp.broadcast_to(jnp.arange(SEQ, dtype=jnp.int32), (BATCH, SEQ)),
        'norm_attn': gain(ks[1], (Ld, D_MODEL)),
        'w_in': nrm(ks[2], (Ld, D_MODEL, IN_COLS), D_MODEL),
        'b_gate': bias(ks[3], (Ld, 2, D_MODEL), 0.02),
        'norm_ckv': gain(ks[4], (Ld, MLA_KV_RANK)),
        'w_ukv': nrm(ks[5], (Ld, MLA_KV_RANK, MLA_HEADS * (MLA_NOPE + MLA_V)), MLA_KV_RANK),
        'q_norm_mla': gain(ks[6], (Ld, MLA_NOPE + MLA_ROPE)),
        'k_norm_mla': gain(ks[7], (Ld, MLA_NOPE + MLA_ROPE)),
        'q_norm_dil': gain(ks[8], (Ld, N_DIL_GROUPS, DIL_HEAD_DIM)),
        'k_norm_dil': gain(ks[9], (Ld, N_DIL_GROUPS, DIL_HEAD_DIM)),
        'w_o_mla': nrm(ks[10], (Ld, MLA_HEADS * MLA_V, D_MODEL), MLA_HEADS * MLA_V),
        'w_o_dil': nrm(ks[11], (Ld, DIL_HEADS * DIL_HEAD_DIM, D_MODEL), DIL_HEADS * DIL_HEAD_DIM),
        'w_out': nrm(ks[12], (Ld, D_MODEL, D_MODEL), D_MODEL),
        'norm_ffn': gain(ks[13], (Ld, D_MODEL)),
        'w_router_group': nrm(ks[14], (Ld, D_MODEL, N_EXPERT_GROUPS), D_MODEL),
        'b_router_group': bias(ks[15], (Ld, N_EXPERT_GROUPS), 0.01),
        'w_router_expert': nrm(ks[16], (Ld, D_MODEL, N_EXPERTS), D_MODEL),
        'b_router_expert': bias(ks[17], (Ld, N_EXPERTS), 0.01),
        'w_gate': nrm(ks[18], (Ld, N_EXPERTS, D_MODEL, D_FF_EXPERT), D_MODEL),
        'w_up': nrm(ks[19], (Ld, N_EXPERTS, D_MODEL, D_FF_EXPERT), D_MODEL),
        'w_down': nrm(ks[20], (Ld, N_EXPERTS, D_FF_EXPERT, D_MODEL), D_FF_EXPERT),
    }


def reference(x, positions, norm_attn, w_in, b_gate, norm_ckv, w_ukv, q_norm_mla, k_norm_mla,
              q_norm_dil, k_norm_dil, w_o_mla, w_o_dil, w_out, norm_ffn, w_router_group,
              b_router_group, w_router_expert, b_router_expert, w_gate, w_up, w_down):
    B, S, D = x.shape
    splits = [MLA_Q_COLS, MLA_Q_COLS + MLA_KV_RANK, MLA_Q_COLS + MLA_KV_RANK + MLA_ROPE,
              MLA_Q_COLS + MLA_KV_RANK + MLA_ROPE + DIL_COLS]
    for l in range(DEPTH):
        h = rms_norm(x, norm_attn[l])
        proj = h @ w_in[l]
        q_mla, ckv, k_rope, qkv_dil, gate_pre = jnp.split(proj, splits, axis=-1)
        o_mla = mla_branch(q_mla, ckv, k_rope, positions, norm_ckv[l], w_ukv[l],
                           q_norm_mla[l], k_norm_mla[l])
        o_dil = dilated_branch(qkv_dil, q_norm_dil[l], k_norm_dil[l])
        gates = jax.nn.sigmoid(gate_pre.reshape(B, S, 2, D) + b_gate[l])
        merged = gates[:, :, 0] * (o_mla @ w_o_mla[l]) + gates[:, :, 1] * (o_dil @ w_o_dil[l])
        x = x + merged @ w_out[l]
        h2 = rms_norm(x, norm_ffn[l])
        x = x + hier_moe(h2, w_router_group[l], b_router_group[l], w_router_expert[l],
                         b_router_expert[l], w_gate[l], w_up[l], w_down[l])
    return x
```

```python
import functools
import math

import jax
import jax.numpy as jnp
import numpy as np
from jax import lax
from jax.experimental import pallas as pl
from jax.experimental.pallas import tpu as pltpu

F32 = jnp.float32
BF16 = jnp.bfloat16
EPS = 1e-6
NEG = -1e30

LANES = 128
VMEM_LIMIT = 56 * 1024 * 1024

MLA_HEADS = 8
MLA_NOPE = 64
MLA_ROPE = 32
MLA_QK = MLA_NOPE + MLA_ROPE
MLA_V = 64
MLA_KV_RANK = 256
ROPE_THETA = 10000.0
DIL_PATTERNS = ((128, 1), (512, 4), (2048, 16))
N_DIL_GROUPS = 3
DIL_HEADS = 8
DIL_HEAD_DIM = 64
DIL_GROUP_COLS = DIL_HEADS * DIL_HEAD_DIM
BAND = 128
N_EXPERT_GROUPS = 4
EXPERTS_PER_GROUP = 8
N_EXPERTS = N_EXPERT_GROUPS * EXPERTS_PER_GROUP
D_FF_EXPERT = 256

_HI = lax.Precision.HIGHEST


def _params(n_axes):
    return pltpu.CompilerParams(dimension_semantics=("arbitrary",) * n_axes,
                                vmem_limit_bytes=VMEM_LIMIT)


def _const_spec(shape):
    nd = len(shape)
    return pl.BlockSpec(shape, lambda *_: (0,) * nd)


def _rms(x, gain):
    return x * lax.rsqrt(jnp.mean(x * x, axis=-1, keepdims=True) + EPS) * gain


def _sigmoid(x):
    return 1.0 / (1.0 + jnp.exp(-x))


def _inproj_body(x_ref, g_ref, wq_ref, wc_ref, wd_ref, q_ref, c_ref, d_ref):
    h = _rms(x_ref[...], g_ref[...]).astype(BF16)
    q_ref[...] = jnp.dot(h, wq_ref[...], preferred_element_type=F32).astype(BF16)
    c_ref[...] = jnp.dot(h, wc_ref[...], preferred_element_type=F32)
    d_ref[...] = jnp.dot(h, wd_ref[...], preferred_element_type=F32).astype(BF16)


def _inproj(x2, g, wq, wc, wd, tm):
    T, D = x2.shape
    nq, nc, nd = wq.shape[1], wc.shape[1], wd.shape[1]
    return pl.pallas_call(
        _inproj_body,
        grid=(T // tm,),
        in_specs=[pl.BlockSpec((tm, D), lambda i: (i, 0)),
                  _const_spec((1, D)), _const_spec((D, nq)), _const_spec((D, nc)), _const_spec((D, nd))],
        out_specs=[pl.BlockSpec((tm, nq), lambda i: (i, 0)),
                   pl.BlockSpec((tm, nc), lambda i: (i, 0)),
                   pl.BlockSpec((tm, nd), lambda i: (i, 0))],
        out_shape=[jax.ShapeDtypeStruct((T, nq), BF16),
                   jax.ShapeDtypeStruct((T, nc), F32),
                   jax.ShapeDtypeStruct((T, nd), BF16)],
        compiler_params=_params(1),
        name="inproj",
    )(x2, g, wq, wc, wd)


def _mla_prep_body(q_ref, c_ref, cs_ref, gq_ref, gkn_ref, gkr_ref, gc_ref, wukv_ref,
                   qf_ref, kf_ref, v_ref):
    lane = lax.broadcasted_iota(jnp.int32, (1, LANES), 1)
    in_qk = lane < MLA_QK
    mid = jnp.logical_and(lane >= MLA_NOPE, lane < MLA_QK)
    hi = lane >= MLA_QK
    cs = cs_ref[...]
    qmul = cs * gq_ref[...]
    for h in range(MLA_HEADS):
        sl = slice(h * LANES, (h + 1) * LANES)
        qh = q_ref[:, sl].astype(F32)
        ssq = jnp.sum(jnp.where(in_qk, qh * qh, 0.0), axis=-1, keepdims=True)
        r = lax.rsqrt(ssq * (1.0 / MLA_QK) + EPS)
        qf_ref[:, sl] = (qh * qmul * r).astype(BF16)

    ckv = c_ref[:, :MLA_KV_RANK]
    kr = c_ref[:, MLA_KV_RANK:]
    cn = _rms(ckv, gc_ref[...]).astype(BF16)
    kv = jnp.dot(cn, wukv_ref[...], preferred_element_type=F32)
    xr = kr * (cs * gkr_ref[...])
    rk2 = xr + jnp.where(mid, pltpu.roll(xr, 96, 1), jnp.where(hi, pltpu.roll(xr, 32, 1), 0.0))
    ssq_r = jnp.sum(jnp.where(mid, kr * kr, 0.0), axis=-1, keepdims=True)
    gkn = gkn_ref[...]
    for h in range(MLA_HEADS):
        sl = slice(h * LANES, (h + 1) * LANES)
        kn = kv[:, sl]
        ssq = jnp.sum(kn * kn, axis=-1, keepdims=True) + ssq_r
        r = lax.rsqrt(ssq * (1.0 / MLA_QK) + EPS)
        kf_ref[:, sl] = ((kn * gkn + rk2) * r).astype(BF16)
    v_ref[...] = kv[:, MLA_HEADS * LANES:].astype(BF16)


def _mla_prep(q1, c1, cs, gq, gkn, gkr, gc, wukv, tm):
    T = q1.shape[0]
    nq, nc, nkv = q1.shape[1], c1.shape[1], wukv.shape[1]
    nv = MLA_HEADS * MLA_V
    return pl.pallas_call(
        _mla_prep_body,
        grid=(T // tm,),
        in_specs=[pl.BlockSpec((tm, nq), lambda i: (i, 0)),
                  pl.BlockSpec((tm, nc), lambda i: (i, 0)),
                  pl.BlockSpec((tm, LANES), lambda i: (i, 0)),
                  _const_spec((1, LANES)), _const_spec((1, LANES)), _const_spec((1, LANES)),
                  _const_spec((1, MLA_KV_RANK)), _const_spec((MLA_KV_RANK, nkv))],
        out_specs=[pl.BlockSpec((tm, nq), lambda i: (i, 0)),
                   pl.BlockSpec((tm, nq), lambda i: (i, 0)),
                   pl.BlockSpec((tm, nv), lambda i: (i, 0))],
        out_shape=[jax.ShapeDtypeStruct((T, nq), BF16),
                   jax.ShapeDtypeStruct((T, nq), BF16),
                   jax.ShapeDtypeStruct((T, nv), BF16)],
        compiler_params=_params(1),
        name="mla_prep",
    )(q1, c1, cs, gq, gkn, gkr, gc, wukv)


def _mla_attn_body(q_ref, k_ref, v_ref, o_ref, *, tq):
    S = q_ref.shape[0]
    nq = S // tq
    row = lax.broadcasted_iota(jnp.int32, (tq, tq), 0)
    col = lax.broadcasted_iota(jnp.int32, (tq, tq), 1)
    causal = col <= row
    lane = lax.broadcasted_iota(jnp.int32, (1, LANES), 1)
    first = lane < MLA_V
    for qi in range(nq):
        rows = slice(qi * tq, (qi + 1) * tq)
        outs = []
        for hh in range(2):
            hs = slice(hh * LANES, (hh + 1) * LANES)
            q = q_ref[rows, hs]
            m = jnp.full((tq, 1), -jnp.inf, F32)
            l = jnp.zeros((tq, 1), F32)
            acc = jnp.zeros((tq, LANES), F32)
            for ki in range(qi + 1):
                krows = slice(ki * tq, (ki + 1) * tq)
                s = lax.dot_general(q, k_ref[krows, hs], (((1,), (1,)), ((), ())),
                                    preferred_element_type=F32)
                if ki == qi:
                    s = jnp.where(causal, s, NEG)
                m_new = jnp.maximum(m, jnp.max(s, axis=-1, keepdims=True))
                p = jnp.exp(s - m_new)
                alpha = jnp.exp(m - m_new)
                l = alpha * l + jnp.sum(p, axis=-1, keepdims=True)
                acc = alpha * acc + jnp.dot(p.astype(BF16), v_ref[krows, :], preferred_element_type=F32)
                m = m_new
            outs.append(acc * (1.0 / l))
        o_ref[rows, :] = jnp.where(first, outs[0], outs[1]).astype(BF16)


def _mla_attn(qf, kf, v, B, S, tq):
    nq = qf.shape[1]
    pairs = MLA_HEADS // 2
    q3 = qf.reshape(B, S, nq)
    k3 = kf.reshape(B, S, nq)
    v3 = v.reshape(B, S, MLA_HEADS * MLA_V)
    out = pl.pallas_call(
        functools.partial(_mla_attn_body, tq=tq),
        grid=(B, pairs),
        in_specs=[pl.BlockSpec((None, S, 2 * LANES), lambda b, p: (b, 0, p)),
                  pl.BlockSpec((None, S, 2 * LANES), lambda b, p: (b, 0, p)),
                  pl.BlockSpec((None, S, LANES), lambda b, p: (b, 0, p))],
        out_specs=pl.BlockSpec((None, S, LANES), lambda b, p: (b, 0, p)),
        out_shape=jax.ShapeDtypeStruct((B, S, MLA_HEADS * MLA_V), BF16),
        compiler_params=_params(2),
        name="mla_attn",
    )(q3, k3, v3)
    return out.reshape(B * S, MLA_HEADS * MLA_V)


def _dil_attn_body(q_ref, k_ref, v_ref, gq_ref, gk_ref, o_ref, lse_ref, qn_scr, kn_scr, bias_scr,
                   *, dil, L):
    lane = lax.broadcasted_iota(jnp.int32, (1, LANES), 1)
    lo = lane < DIL_HEAD_DIM
    inv_dh = 1.0 / DIL_HEAD_DIM

    for pb in range(DIL_HEADS // 2):
        sl = slice(pb * LANES, (pb + 1) * LANES)
        x = k_ref[:, sl].astype(F32)
        x2 = x * x
        s_lo = jnp.sum(jnp.where(lo, x2, 0.0), axis=-1, keepdims=True)
        s_hi = jnp.sum(jnp.where(lo, 0.0, x2), axis=-1, keepdims=True)
        r = jnp.where(lo, lax.rsqrt(s_lo * inv_dh + EPS), lax.rsqrt(s_hi * inv_dh + EPS))
        kn_scr[:, sl] = (x * r * gk_ref[...]).astype(BF16)
        x = q_ref[:, sl].astype(F32)
        x2 = x * x
        s_lo = jnp.sum(jnp.where(lo, x2, 0.0), axis=-1, keepdims=True)
        s_hi = jnp.sum(jnp.where(lo, 0.0, x2), axis=-1, keepdims=True)
        xg = x * gq_ref[...]
        qn_scr[:, (2 * pb) * LANES:(2 * pb + 1) * LANES] = jnp.where(
            lo, xg * lax.rsqrt(s_lo * inv_dh + EPS), 0.0).astype(BF16)
        qn_scr[:, (2 * pb + 1) * LANES:(2 * pb + 2) * LANES] = jnp.where(
            lo, 0.0, xg * lax.rsqrt(s_hi * inv_dh + EPS)).astype(BF16)

    qi = lax.broadcasted_iota(jnp.int32, (BAND, 2 * BAND), 0)
    kj = lax.broadcasted_iota(jnp.int32, (BAND, 2 * BAND), 1)
    delta = qi + BAND - kj
    valid = jnp.logical_and(delta >= 0, delta <= BAND)
    dist = (delta * dil).astype(F32)
    for h in range(DIL_HEADS):
        slope = 2.0 ** (-8.0 * (h + 1) / DIL_HEADS)
        bias_scr[h] = jnp.where(valid, -slope * dist, NEG)

    def q_block(qs, ks, nk):
        lses = jnp.zeros((BAND, LANES), F32)
        for pb in range(DIL_HEADS // 2):
            sl = slice(pb * LANES, (pb + 1) * LANES)
            kp = kn_scr[pl.ds(ks, nk), sl]
            vp = v_ref[pl.ds(ks, nk), sl]
            outs = []
            for hh in range(2):
                h = 2 * pb + hh
                qh = qn_scr[pl.ds(qs, BAND), h * LANES:(h + 1) * LANES]
                s = lax.dot_general(qh, kp, (((1,), (1,)), ((), ())), preferred_element_type=F32)
                s = s + bias_scr[h, :, 2 * BAND - nk:]
                m = jnp.max(s, axis=-1, keepdims=True)
                p = jnp.exp(s - m)
                l = jnp.sum(p, axis=-1, keepdims=True)
                o = jnp.dot(p.astype(BF16), vp, preferred_element_type=F32)
                outs.append(o * (1.0 / l))
                lses = jnp.where(lane == h, m + jnp.log(l), lses)
            o_ref[pl.ds(qs, BAND), sl] = jnp.where(lo, outs[0], outs[1]).astype(BF16)
        lse_ref[pl.ds(qs, BAND), :] = lses

    q_block(0, 0, BAND)

    nb = L // BAND
    if nb > 1:
        def body(i, carry):
            qs = pl.multiple_of(i * BAND, BAND)
            ks = pl.multiple_of((i - 1) * BAND, BAND)
            q_block(qs, ks, 2 * BAND)
            return carry
        lax.fori_loop(1, nb, body, 0)


def _dil_attn(dproj, gq2, gk2, B, S, g):
    window, dil = DIL_PATTERNS[g]
    assert window // dil == BAND
    L = S // dil
    assert L % BAND == 0
    ncols = dproj.shape[1]
    blocks = ncols // DIL_GROUP_COLS
    d3 = dproj.reshape(B, L, dil * ncols)

    def in_spec(which):
        return pl.BlockSpec((None, L, DIL_GROUP_COLS),
                            lambda b, r: (b, 0, r * blocks + which * N_DIL_GROUPS + g))

    o, lse = pl.pallas_call(
        functools.partial(_dil_attn_body, dil=dil, L=L),
        grid=(B, dil),
        in_specs=[in_spec(0), in_spec(1), in_spec(2), _const_spec((1, LANES)), _const_spec((1, LANES))],
        out_specs=[pl.BlockSpec((None, L, DIL_GROUP_COLS), lambda b, r: (b, 0, r)),
                   pl.BlockSpec((None, L, LANES), lambda b, r: (b, 0, r))],
        out_shape=[jax.ShapeDtypeStruct((B, L, dil * DIL_GROUP_COLS), BF16),
                   jax.ShapeDtypeStruct((B, L, dil * LANES), F32)],
        scratch_shapes=[pltpu.VMEM((L, DIL_HEADS * LANES), BF16),
                        pltpu.VMEM((L, DIL_GROUP_COLS), BF16),
                        pltpu.VMEM((DIL_HEADS, BAND, 2 * BAND), F32)],
        compiler_params=_params(2),
        name=f"dil_attn_g{g}",
    )(d3, d3, d3, gq2, gk2)
    return o.reshape(B * S, DIL_GROUP_COLS), lse.reshape(B * S, LANES)


def _outproj_body(x_ref, g1_ref, wg_ref, bg_ref, om_ref, od0_ref, od1_ref, od2_ref,
                  ls0_ref, ls1_ref, ls2_ref, ex_ref, wom_ref, wod_ref, wout_ref, g2_ref,
                  wrg_ref, brg_ref, wre_ref, bre_ref, x1_ref, h2_ref, comb_ref):
    D = x_ref.shape[1]
    x = x_ref[...]
    h = _rms(x, g1_ref[...]).astype(BF16)
    gp = jnp.dot(h, wg_ref[...], preferred_element_type=F32) + bg_ref[...]
    gate_a = _sigmoid(gp[:, :D])
    gate_b = _sigmoid(gp[:, D:])

    l0, l1, l2 = ls0_ref[...], ls1_ref[...], ls2_ref[...]
    mx = jnp.maximum(jnp.maximum(l0, l1), l2)
    e0, e1, e2 = jnp.exp(l0 - mx), jnp.exp(l1 - mx), jnp.exp(l2 - mx)
    inv = 1.0 / (e0 + e1 + e2)
    ex = ex_ref[...]
    od = (jnp.dot(e0 * inv, ex, precision=_HI, preferred_element_type=F32) * od0_ref[...].astype(F32)
          + jnp.dot(e1 * inv, ex, precision=_HI, preferred_element_type=F32) * od1_ref[...].astype(F32)
          + jnp.dot(e2 * inv, ex, precision=_HI, preferred_element_type=F32) * od2_ref[...].astype(F32))

    a = jnp.dot(om_ref[...], wom_ref[...], preferred_element_type=F32)
    b = jnp.dot(od.astype(BF16), wod_ref[...], preferred_element_type=F32)
    merged = gate_a * a + gate_b * b
    x1 = x + jnp.dot(merged.astype(BF16), wout_ref[...], preferred_element_type=F32)
    x1_ref[...] = x1
    h2 = _rms(x1, g2_ref[...])
    h2_ref[...] = h2.astype(BF16)

    lane = lax.broadcasted_iota(jnp.int32, (1, LANES), 1)
    lane_f = lane.astype(F32)
    big = float(LANES)
    lg = jnp.dot(h2, wrg_ref[...], precision=_HI, preferred_element_type=F32) + brg_ref[...]
    lg = jnp.where(lane < N_EXPERT_GROUPS, lg, -jnp.inf)
    gmax = jnp.max(lg, axis=-1, keepdims=True)
    gi = jnp.min(jnp.where(lg == gmax, lane_f, big), axis=-1, keepdims=True)
    g_p = 1.0 / jnp.sum(jnp.exp(lg - gmax), axis=-1, keepdims=True)
    le = jnp.dot(h2, wre_ref[...], precision=_HI, preferred_element_type=F32) + bre_ref[...]
    lo_e = gi * EXPERTS_PER_GROUP
    sel = jnp.logical_and(lane_f >= lo_e, lane_f < lo_e + EXPERTS_PER_GROUP)
    els = jnp.where(sel, le, -jnp.inf)
    m1 = jnp.max(els, axis=-1, keepdims=True)
    i1 = jnp.min(jnp.where(els == m1, lane_f, big), axis=-1, keepdims=True)
    els2 = jnp.where(lane_f == i1, -jnp.inf, els)
    m2 = jnp.max(els2, axis=-1, keepdims=True)
    i2 = jnp.min(jnp.where(els2 == m2, lane_f, big), axis=-1, keepdims=True)
    t = jnp.exp(m2 - m1)
    w1 = g_p / (1.0 + t)
    w2 = w1 * t
    comb_ref[...] = jnp.where(lane_f == i1, w1, jnp.where(lane_f == i2, w2, 0.0))


def _outproj(x2, g1, wg, bg, om, ods, lss, ex, wom, wod, wout, g2, wrg, brg, wre, bre, tm):
    T, D = x2.shape
    row = lambda n: pl.BlockSpec((tm, n), lambda i: (i, 0))
    return pl.pallas_call(
        _outproj_body,
        grid=(T // tm,),
        in_specs=[row(D), _const_spec((1, D)), _const_spec(wg.shape), _const_spec((1, 2 * D)),
                  row(om.shape[1]), row(DIL_GROUP_COLS), row(DIL_GROUP_COLS), row(DIL_GROUP_COLS),
                  row(LANES), row(LANES), row(LANES), _const_spec(ex.shape),
                  _const_spec(wom.shape), _const_spec(wod.shape), _const_spec(wout.shape),
                  _const_spec((1, D)), _const_spec(wrg.shape), _const_spec((1, LANES)),
                  _const_spec(wre.shape), _const_spec((1, LANES))],
        out_specs=[row(D), row(D), row(LANES)],
        out_shape=[jax.ShapeDtypeStruct((T, D), F32),
                   jax.ShapeDtypeStruct((T, D), BF16),
                   jax.ShapeDtypeStruct((T, LANES), F32)],
        compiler_params=_params(1),
        name="outproj_router",
    )(x2, g1, wg, bg, om, ods[0], ods[1], ods[2], lss[0], lss[1], lss[2], ex,
      wom, wod, wout, g2, wrg, brg, wre, bre)


def _moe_dense_body(h2_ref, comb_ref, x1_ref, wg_ref, wu_ref, wd_ref, o_ref):
    e = pl.program_id(1)

    @pl.when(e == 0)
    def _():
        o_ref[...] = x1_ref[...]

    h = h2_ref[...]
    a = jnp.dot(h, wg_ref[...], preferred_element_type=F32)
    b = jnp.dot(h, wu_ref[...], preferred_element_type=F32)
    he = (a * _sigmoid(a) * b).astype(BF16)
    lane = lax.broadcasted_iota(jnp.int32, (1, LANES), 1)
    c = jnp.sum(jnp.where(lane == e, comb_ref[...], 0.0), axis=-1, keepdims=True)
    o_ref[...] += c * jnp.dot(he, wd_ref[...], preferred_element_type=F32)


def _moe_dense(h2, comb, x1, wg, wu, wd, tm):
    T, D = x1.shape
    E, _, F = wg.shape
    return pl.pallas_call(
        _moe_dense_body,
        grid=(T // tm, E),
        in_specs=[pl.BlockSpec((tm, D), lambda i, e: (i, 0)),
                  pl.BlockSpec((tm, LANES), lambda i, e: (i, 0)),
                  pl.BlockSpec((tm, D), lambda i, e: (i, 0)),
                  pl.BlockSpec((None, D, F), lambda i, e: (e, 0, 0)),
                  pl.BlockSpec((None, D, F), lambda i, e: (e, 0, 0)),
                  pl.BlockSpec((None, F, D), lambda i, e: (e, 0, 0))],
        out_specs=pl.BlockSpec((tm, D), lambda i, e: (i, 0)),
        out_shape=jax.ShapeDtypeStruct((T, D), F32),
        compiler_params=_params(2),
        name="moe_dense",
    )(h2, comb, x1, wg, wu, wd)


def _rot_half_cols(w):
    half = w.shape[-1] // 2
    return jnp.concatenate([-w[..., half:], w[..., :half]], axis=-1)


def _swap_halves(g):
    half = g.shape[-1] // 2
    return jnp.concatenate([g[..., half:], g[..., :half]], axis=-1)


def kernel(x, positions, norm_attn, w_in, b_gate, norm_ckv, w_ukv, q_norm_mla, k_norm_mla, q_norm_dil, k_norm_dil, w_o_mla, w_o_dil, w_out, norm_ffn, w_router_group, b_router_group, w_router_expert, b_router_expert, w_gate, w_up, w_down):
    B, S, D = x.shape
    T = B * S
    depth = norm_attn.shape[0]
    q_cols = MLA_HEADS * MLA_QK
    c0 = q_cols
    c1 = c0 + MLA_KV_RANK
    c2 = c1 + MLA_ROPE
    c3 = c2 + 3 * N_DIL_GROUPS * DIL_GROUP_COLS

    half = MLA_ROPE // 2
    freqs = ROPE_THETA ** (-jnp.arange(half, dtype=F32) / half)
    ang = positions.astype(F32).reshape(T, 1) * freqs
    cos, sin = jnp.cos(ang), jnp.sin(ang)
    cs = jnp.concatenate([jnp.ones((T, MLA_NOPE), F32), cos, cos, sin, sin], axis=-1)

    ex = jnp.asarray(np.kron(np.eye(LANES, DIL_HEADS, dtype=np.float32).reshape(LANES, DIL_HEADS),
                             np.ones((1, DIL_HEAD_DIM), np.float32)))

    xcur = x.reshape(T, D)
    for l in range(depth):
        w = w_in[l]
        wq = w[:, :c0].reshape(D, MLA_HEADS, MLA_QK)
        wq = jnp.concatenate([wq, _rot_half_cols(wq[:, :, MLA_NOPE:])], axis=-1).reshape(D, MLA_HEADS * LANES)
        wkr = w[:, c1:c2]
        wc = jnp.concatenate([w[:, c0:c1], jnp.zeros((D, MLA_NOPE), F32), wkr, _rot_half_cols(wkr)], axis=-1)
        wd = w[:, c2:c3]
        wgate = w[:, c3:]

        wukv = w_ukv[l].reshape(MLA_KV_RANK, MLA_HEADS, MLA_NOPE + MLA_V)
        wk = jnp.concatenate([wukv[:, :, :MLA_NOPE], jnp.zeros((MLA_KV_RANK, MLA_HEADS, LANES - MLA_NOPE), F32)], axis=-1)
        wukv2 = jnp.concatenate([wk.reshape(MLA_KV_RANK, MLA_HEADS * LANES),
                                 wukv[:, :, MLA_NOPE:].reshape(MLA_KV_RANK, MLA_HEADS * MLA_V)], axis=-1)

        qn, kn = q_norm_mla[l], k_norm_mla[l]
        scale = MLA_QK ** -0.5
        gq = (jnp.concatenate([qn, _swap_halves(qn[MLA_NOPE:])]) * scale).reshape(1, LANES)
        gkn = jnp.concatenate([kn[:MLA_NOPE], jnp.zeros((LANES - MLA_NOPE,), F32)]).reshape(1, LANES)
        gkr = jnp.concatenate([jnp.zeros((MLA_NOPE,), F32), kn[MLA_NOPE:], _swap_halves(kn[MLA_NOPE:])]).reshape(1, LANES)

        q1, cc, dproj = _inproj(xcur, norm_attn[l].reshape(1, D), wq.astype(BF16), wc.astype(BF16),
                                wd.astype(BF16), tm=512)
        qf, kf, v = _mla_prep(q1, cc, cs, gq, gkn, gkr, norm_ckv[l].reshape(1, MLA_KV_RANK),
                              wukv2.astype(BF16), tm=512)
        o_mla = _mla_attn(qf, kf, v, B, S, tq=512)

        ods, lss = [], []
        for g in range(N_DIL_GROUPS):
            gq2 = (jnp.tile(q_norm_dil[l, g], 2) * DIL_HEAD_DIM ** -0.5).reshape(1, LANES)
            gk2 = jnp.tile(k_norm_dil[l, g], 2).reshape(1, LANES)
            o, lse = _dil_attn(dproj, gq2, gk2, B, S, g)
            ods.append(o)
            lss.append(lse)

        pad_cols = lambda a: jnp.pad(a, ((0, 0), (0, LANES - a.shape[1])))
        x1, h2, comb = _outproj(
            xcur, norm_attn[l].reshape(1, D), wgate.astype(BF16), b_gate[l].reshape(1, 2 * D),
            o_mla, ods, lss, ex, w_o_mla[l].astype(BF16), w_o_dil[l].astype(BF16), w_out[l].astype(BF16),
            norm_ffn[l].reshape(1, D), pad_cols(w_router_group[l]), pad_cols(b_router_group[l].reshape(1, -1)),
            pad_cols(w_router_expert[l]), pad_cols(b_router_expert[l].reshape(1, -1)), tm=256)
        xcur = _moe_dense(h2, comb, x1, w_gate[l].astype(BF16), w_up[l].astype(BF16), w_down[l].astype(BF16),
                          tm=1024)
    return xcur.reshape(B, S, D)
```

```python
import functools
import math

import jax
import jax.numpy as jnp
import numpy as np
from jax import lax
from jax.experimental import pallas as pl
from jax.experimental.pallas import tpu as pltpu

F32 = jnp.float32
BF16 = jnp.bfloat16
EPS = 1e-6
NEG = -1e30

LANES = 128
VMEM_LIMIT = 56 * 1024 * 1024

MLA_HEADS = 8
MLA_NOPE = 64
MLA_ROPE = 32
MLA_QK = MLA_NOPE + MLA_ROPE
MLA_V = 64
MLA_KV_RANK = 256
ROPE_THETA = 10000.0
DIL_PATTERNS = ((128, 1), (512, 4), (2048, 16))
N_DIL_GROUPS = 3
DIL_HEADS = 8
DIL_HEAD_DIM = 64
DIL_GROUP_COLS = DIL_HEADS * DIL_HEAD_DIM
BAND = 128
N_EXPERT_GROUPS = 4
EXPERTS_PER_GROUP = 8
N_EXPERTS = N_EXPERT_GROUPS * EXPERTS_PER_GROUP
D_FF_EXPERT = 256

_HI = lax.Precision.HIGHEST


def _params(n_axes):
    return pltpu.CompilerParams(dimension_semantics=("arbitrary",) * n_axes,
                                vmem_limit_bytes=VMEM_LIMIT)


def _const_spec(shape):
    nd = len(shape)
    return pl.BlockSpec(shape, lambda *_: (0,) * nd)


def _rms(x, gain):
    return x * lax.rsqrt(jnp.mean(x * x, axis=-1, keepdims=True) + EPS) * gain


def _sigmoid(x):
    return 1.0 / (1.0 + jnp.exp(-x))


def _inproj_body(x_ref, g_ref, wq_ref, wc_ref, wd_ref, q_ref, c_ref, d0_ref, d12_ref):
    h = _rms(x_ref[...], g_ref[...]).astype(BF16)
    q_ref[...] = jnp.dot(h, wq_ref[...], preferred_element_type=F32).astype(BF16)
    c_ref[...] = jnp.dot(h, wc_ref[...], preferred_element_type=F32)
    d = jnp.dot(h, wd_ref[...], preferred_element_type=F32)
    n0 = d0_ref.shape[0]
    for j in range(n0):
        d0_ref[j] = d[:, j * LANES:(j + 1) * LANES].astype(BF16)
    for j in range(d12_ref.shape[0]):
        d12_ref[j] = d[:, (n0 + j) * LANES:(n0 + j + 1) * LANES]


def _inproj(x2, g, wq, wc, wd, tm):
    T, D = x2.shape
    nq, nc, nd = wq.shape[1], wc.shape[1], wd.shape[1]
    n0 = 3 * DIL_GROUP_COLS // LANES
    n12 = nd // LANES - n0
    return pl.pallas_call(
        _inproj_body,
        grid=(T // tm,),
        in_specs=[pl.BlockSpec((tm, D), lambda i: (i, 0)),
                  _const_spec((1, D)), _const_spec((D, nq)), _const_spec((D, nc)), _const_spec((D, nd))],
        out_specs=[pl.BlockSpec((tm, nq), lambda i: (i, 0)),
                   pl.BlockSpec((tm, nc), lambda i: (i, 0)),
                   pl.BlockSpec((n0, tm, LANES), lambda i: (0, i, 0)),
                   pl.BlockSpec((n12, tm, LANES), lambda i: (0, i, 0))],
        out_shape=[jax.ShapeDtypeStruct((T, nq), BF16),
                   jax.ShapeDtypeStruct((T, nc), F32),
                   jax.ShapeDtypeStruct((n0, T, LANES), BF16),
                   jax.ShapeDtypeStruct((n12, T, LANES), F32)],
        compiler_params=_params(1),
        name="inproj",
    )(x2, g, wq, wc, wd)


def _mla_prep_body(q_ref, c_ref, cs_ref, gq_ref, gkn_ref, gkr_ref, gc_ref, wukv_ref,
                   qf_ref, kf_ref, v_ref):
    lane = lax.broadcasted_iota(jnp.int32, (1, LANES), 1)
    in_qk = lane < MLA_QK
    mid = jnp.logical_and(lane >= MLA_NOPE, lane < MLA_QK)
    hi = lane >= MLA_QK
    cs = cs_ref[...]
    qmul = cs * gq_ref[...]
    for h in range(MLA_HEADS):
        sl = slice(h * LANES, (h + 1) * LANES)
        qh = q_ref[:, sl].astype(F32)
        ssq = jnp.sum(jnp.where(in_qk, qh * qh, 0.0), axis=-1, keepdims=True)
        r = lax.rsqrt(ssq * (1.0 / MLA_QK) + EPS)
        qf_ref[:, sl] = (qh * qmul * r).astype(BF16)

    ckv = c_ref[:, :MLA_KV_RANK]
    kr = c_ref[:, MLA_KV_RANK:]
    cn = _rms(ckv, gc_ref[...]).astype(BF16)
    kv = jnp.dot(cn, wukv_ref[...], preferred_element_type=F32)
    xr = kr * (cs * gkr_ref[...])
    rk2 = xr + jnp.where(mid, pltpu.roll(xr, 96, 1), jnp.where(hi, pltpu.roll(xr, 32, 1), 0.0))
    ssq_r = jnp.sum(jnp.where(mid, kr * kr, 0.0), axis=-1, keepdims=True)
    gkn = gkn_ref[...]
    for h in range(MLA_HEADS):
        sl = slice(h * LANES, (h + 1) * LANES)
        kn = kv[:, sl]
        ssq = jnp.sum(kn * kn, axis=-1, keepdims=True) + ssq_r
        r = lax.rsqrt(ssq * (1.0 / MLA_QK) + EPS)
        kf_ref[:, sl] = ((kn * gkn + rk2) * r).astype(BF16)
    v_ref[...] = kv[:, MLA_HEADS * LANES:].astype(BF16)


def _mla_prep(q1, c1, cs, gq, gkn, gkr, gc, wukv, tm):
    T = q1.shape[0]
    nq, nc, nkv = q1.shape[1], c1.shape[1], wukv.shape[1]
    nv = MLA_HEADS * MLA_V
    return pl.pallas_call(
        _mla_prep_body,
        grid=(T // tm,),
        in_specs=[pl.BlockSpec((tm, nq), lambda i: (i, 0)),
                  pl.BlockSpec((tm, nc), lambda i: (i, 0)),
                  pl.BlockSpec((tm, LANES), lambda i: (i, 0)),
                  _const_spec((1, LANES)), _const_spec((1, LANES)), _const_spec((1, LANES)),
                  _const_spec((1, MLA_KV_RANK)), _const_spec((MLA_KV_RANK, nkv))],
        out_specs=[pl.BlockSpec((tm, nq), lambda i: (i, 0)),
                   pl.BlockSpec((tm, nq), lambda i: (i, 0)),
                   pl.BlockSpec((tm, nv), lambda i: (i, 0))],
        out_shape=[jax.ShapeDtypeStruct((T, nq), BF16),
                   jax.ShapeDtypeStruct((T, nq), BF16),
                   jax.ShapeDtypeStruct((T, nv), BF16)],
        compiler_params=_params(1),
        name="mla_prep",
    )(q1, c1, cs, gq, gkn, gkr, gc, wukv)


def _mla_attn_body(q_ref, k_ref, v_ref, o_ref, *, tq):
    S = q_ref.shape[0]
    nq = S // tq
    row = lax.broadcasted_iota(jnp.int32, (tq, tq), 0)
    col = lax.broadcasted_iota(jnp.int32, (tq, tq), 1)
    causal = col <= row
    lane = lax.broadcasted_iota(jnp.int32, (1, LANES), 1)
    first = lane < MLA_V
    for qi in range(nq):
        rows = slice(qi * tq, (qi + 1) * tq)
        outs = []
        for hh in range(2):
            hs = slice(hh * LANES, (hh + 1) * LANES)
            q = q_ref[rows, hs]
            m = jnp.full((tq, 1), -jnp.inf, F32)
            l = jnp.zeros((tq, 1), F32)
            acc = jnp.zeros((tq, LANES), F32)
            for ki in range(qi + 1):
                krows = slice(ki * tq, (ki + 1) * tq)
                s = lax.dot_general(q, k_ref[krows, hs], (((1,), (1,)), ((), ())),
                                    preferred_element_type=F32)
                if ki == qi:
                    s = jnp.where(causal, s, NEG)
                m_new = jnp.maximum(m, jnp.max(s, axis=-1, keepdims=True))
                p = jnp.exp(s - m_new)
                alpha = jnp.exp(m - m_new)
                l = alpha * l + jnp.sum(p, axis=-1, keepdims=True)
                acc = alpha * acc + jnp.dot(p.astype(BF16), v_ref[krows, :], preferred_element_type=F32)
                m = m_new
            outs.append(acc * (1.0 / l))
        o_ref[rows, :] = jnp.where(first, outs[0], outs[1]).astype(BF16)


def _mla_attn(qf, kf, v, B, S, tq):
    nq = qf.shape[1]
    pairs = MLA_HEADS // 2
    q3 = qf.reshape(B, S, nq)
    k3 = kf.reshape(B, S, nq)
    v3 = v.reshape(B, S, MLA_HEADS * MLA_V)
    out = pl.pallas_call(
        functools.partial(_mla_attn_body, tq=tq),
        grid=(B, pairs),
        in_specs=[pl.BlockSpec((None, S, 2 * LANES), lambda b, p: (b, 0, p)),
                  pl.BlockSpec((None, S, 2 * LANES), lambda b, p: (b, 0, p)),
                  pl.BlockSpec((None, S, LANES), lambda b, p: (b, 0, p))],
        out_specs=pl.BlockSpec((None, S, LANES), lambda b, p: (b, 0, p)),
        out_shape=jax.ShapeDtypeStruct((B, S, MLA_HEADS * MLA_V), BF16),
        compiler_params=_params(2),
        name="mla_attn",
    )(q3, k3, v3)
    return out.reshape(B * S, MLA_HEADS * MLA_V)


def _dil_attn_body(q_ref, k_ref, v_ref, gq_ref, gk_ref, o_ref, lse_ref, qn_scr, kn_scr, v_scr, bias_scr,
                   *, dil, L):
    r_sub = pl.program_id(1)
    lane = lax.broadcasted_iota(jnp.int32, (1, LANES), 1)
    lo = lane < DIL_HEAD_DIM
    inv_dh = 1.0 / DIL_HEAD_DIM

    def rows(start, n):
        if dil == 1:
            return pl.ds(start, n)
        return pl.ds(start * dil + r_sub, n, stride=dil)

    for pb in range(DIL_HEADS // 2):
        sl = slice(pb * LANES, (pb + 1) * LANES)
        x = k_ref[pb, rows(0, L), :].astype(F32)
        x2 = x * x
        s_lo = jnp.sum(jnp.where(lo, x2, 0.0), axis=-1, keepdims=True)
        s_hi = jnp.sum(jnp.where(lo, 0.0, x2), axis=-1, keepdims=True)
        r = jnp.where(lo, lax.rsqrt(s_lo * inv_dh + EPS), lax.rsqrt(s_hi * inv_dh + EPS))
        kn_scr[:, sl] = (x * r * gk_ref[...]).astype(BF16)
        x = q_ref[pb, rows(0, L), :].astype(F32)
        x2 = x * x
        s_lo = jnp.sum(jnp.where(lo, x2, 0.0), axis=-1, keepdims=True)
        s_hi = jnp.sum(jnp.where(lo, 0.0, x2), axis=-1, keepdims=True)
        xg = x * gq_ref[...]
        qn_scr[:, (2 * pb) * LANES:(2 * pb + 1) * LANES] = jnp.where(
            lo, xg * lax.rsqrt(s_lo * inv_dh + EPS), 0.0).astype(BF16)
        qn_scr[:, (2 * pb + 1) * LANES:(2 * pb + 2) * LANES] = jnp.where(
            lo, 0.0, xg * lax.rsqrt(s_hi * inv_dh + EPS)).astype(BF16)
        v_scr[:, sl] = v_ref[pb, rows(0, L), :].astype(BF16)

    @pl.when(jnp.logical_and(pl.program_id(0) == 0, r_sub == 0))
    def _():
        qi = lax.broadcasted_iota(jnp.int32, (BAND, 2 * BAND), 0)
        kj = lax.broadcasted_iota(jnp.int32, (BAND, 2 * BAND), 1)
        delta = qi + BAND - kj
        valid = jnp.logical_and(delta >= 0, delta <= BAND)
        dist = (delta * dil).astype(F32)
        for h in range(DIL_HEADS):
            slope = 2.0 ** (-8.0 * (h + 1) / DIL_HEADS)
            bias_scr[h] = jnp.where(valid, -slope * dist, NEG)

    def q_block(qs, ks, nk):
        lses = jnp.zeros((BAND, LANES), F32)
        for pb in range(DIL_HEADS // 2):
            sl = slice(pb * LANES, (pb + 1) * LANES)
            kp = kn_scr[pl.ds(ks, nk), sl]
            vp = v_scr[pl.ds(ks, nk), sl]
            outs = []
            for hh in range(2):
                h = 2 * pb + hh
                qh = qn_scr[pl.ds(qs, BAND), h * LANES:(h + 1) * LANES]
                s = lax.dot_general(qh, kp, (((1,), (1,)), ((), ())), preferred_element_type=F32)
                s = s + bias_scr[h, :, 2 * BAND - nk:]
                m = jnp.max(s, axis=-1, keepdims=True)
                p = jnp.exp(s - m)
                l = jnp.sum(p, axis=-1, keepdims=True)
                o = jnp.dot(p.astype(BF16), vp, preferred_element_type=F32)
                outs.append(o * (1.0 / l))
                lses = jnp.where(lane == h, m + jnp.log(l), lses)
            o_ref[pb, rows(qs, BAND), :] = jnp.where(lo, outs[0], outs[1]).astype(o_ref.dtype)
        lse_ref[rows(qs, BAND), :] = lses

    q_block(0, 0, BAND)

    nb = L // BAND
    if nb > 1:
        def body(i, carry):
            qs = pl.multiple_of(i * BAND, BAND)
            ks = pl.multiple_of((i - 1) * BAND, BAND)
            q_block(qs, ks, 2 * BAND)
            return carry
        lax.fori_loop(1, nb, body, 0)


def _dil_attn(slabs, first_slab_block, gq2, gk2, B, S, g):
    window, dil = DIL_PATTERNS[g]
    assert window // dil == BAND
    L = S // dil
    assert L % BAND == 0
    T = B * S
    pairs = DIL_HEADS // 2

    def in_spec(which):
        return pl.BlockSpec((pairs, S, LANES), lambda b, r: (first_slab_block + which, b, 0))

    return pl.pallas_call(
        functools.partial(_dil_attn_body, dil=dil, L=L),
        grid=(B, dil),
        in_specs=[in_spec(0), in_spec(1), in_spec(2), _const_spec((1, LANES)), _const_spec((1, LANES))],
        out_specs=[pl.BlockSpec((pairs, S, LANES), lambda b, r: (0, b, 0)),
                   pl.BlockSpec((S, LANES), lambda b, r: (b, 0))],
        out_shape=[jax.ShapeDtypeStruct((pairs, T, LANES), slabs.dtype),
                   jax.ShapeDtypeStruct((T, LANES), F32)],
        scratch_shapes=[pltpu.VMEM((L, DIL_HEADS * LANES), BF16),
                        pltpu.VMEM((L, DIL_GROUP_COLS), BF16),
                        pltpu.VMEM((L, DIL_GROUP_COLS), BF16),
                        pltpu.VMEM((DIL_HEADS, BAND, 2 * BAND), F32)],
        compiler_params=_params(2),
        name=f"dil_attn_g{g}",
    )(slabs, slabs, slabs, gq2, gk2)


def _split_bf16(v):
    hi = v.astype(BF16)
    lo = (v - hi.astype(F32)).astype(BF16)
    return hi, lo


def _outproj_body(x_ref, g1_ref, wg_ref, bg_ref, om_ref, od0_ref, od1_ref, od2_ref,
                  ls0_ref, ls1_ref, ls2_ref, ex_ref, wom_ref, wod_ref, wout_ref, g2_ref,
                  wrh_ref, wrl_ref, br_ref, x1_ref, h2_ref, comb_ref):
    D = x_ref.shape[1]
    x = x_ref[...]
    h = _rms(x, g1_ref[...]).astype(BF16)
    gp = jnp.dot(h, wg_ref[...], preferred_element_type=F32) + bg_ref[...]
    gate_a = _sigmoid(gp[:, :D])
    gate_b = _sigmoid(gp[:, D:])

    l0, l1, l2 = ls0_ref[...], ls1_ref[...], ls2_ref[...]
    mx = jnp.maximum(jnp.maximum(l0, l1), l2)
    e0, e1, e2 = jnp.exp(l0 - mx), jnp.exp(l1 - mx), jnp.exp(l2 - mx)
    inv = 1.0 / (e0 + e1 + e2)
    ex = ex_ref[...]

    def spread(w, od_ref):
        hi, lo = _split_bf16(w)
        wide = (jnp.dot(hi, ex, preferred_element_type=F32) + jnp.dot(lo, ex, preferred_element_type=F32))
        o = jnp.concatenate([od_ref[p] for p in range(od_ref.shape[0])], axis=-1)
        return wide * o.astype(F32)

    od = spread(e0 * inv, od0_ref) + spread(e1 * inv, od1_ref) + spread(e2 * inv, od2_ref)

    a = jnp.dot(om_ref[...], wom_ref[...], preferred_element_type=F32)
    b = jnp.dot(od.astype(BF16), wod_ref[...], preferred_element_type=F32)
    merged = gate_a * a + gate_b * b
    x1 = x + jnp.dot(merged.astype(BF16), wout_ref[...], preferred_element_type=F32)
    x1_ref[...] = x1
    h2 = _rms(x1, g2_ref[...])
    h2_ref[...] = h2.astype(BF16)

    lane = lax.broadcasted_iota(jnp.int32, (1, LANES), 1)
    lane_f = lane.astype(F32)
    big = float(LANES)
    h2h, h2l = _split_bf16(h2)
    wrh = wrh_ref[...]
    logits = (jnp.dot(h2h, wrh, preferred_element_type=F32)
              + jnp.dot(h2h, wrl_ref[...], preferred_element_type=F32)
              + jnp.dot(h2l, wrh, preferred_element_type=F32)) + br_ref[...]
    is_group = jnp.logical_and(lane >= N_EXPERTS, lane < N_EXPERTS + N_EXPERT_GROUPS)
    lg = jnp.where(is_group, logits, -jnp.inf)
    gmax = jnp.max(lg, axis=-1, keepdims=True)
    gi = jnp.min(jnp.where(lg == gmax, lane_f, big), axis=-1, keepdims=True) - float(N_EXPERTS)
    g_p = 1.0 / jnp.sum(jnp.exp(lg - gmax), axis=-1, keepdims=True)
    le = logits
    lo_e = gi * EXPERTS_PER_GROUP
    sel = jnp.logical_and(lane_f >= lo_e, lane_f < lo_e + EXPERTS_PER_GROUP)
    els = jnp.where(sel, le, -jnp.inf)
    m1 = jnp.max(els, axis=-1, keepdims=True)
    i1 = jnp.min(jnp.where(els == m1, lane_f, big), axis=-1, keepdims=True)
    els2 = jnp.where(lane_f == i1, -jnp.inf, els)
    m2 = jnp.max(els2, axis=-1, keepdims=True)
    i2 = jnp.min(jnp.where(els2 == m2, lane_f, big), axis=-1, keepdims=True)
    t = jnp.exp(m2 - m1)
    w1 = g_p / (1.0 + t)
    w2 = w1 * t
    comb_ref[...] = jnp.where(lane_f == i1, w1, jnp.where(lane_f == i2, w2, 0.0))


def _outproj(x2, g1, wg, bg, om, ods, lss, ex, wom, wod, wout, g2, wrh, wrl, br, tm):
    T, D = x2.shape
    row = lambda n: pl.BlockSpec((tm, n), lambda i: (i, 0))
    slab = pl.BlockSpec((DIL_HEADS // 2, tm, LANES), lambda i: (0, i, 0))
    return pl.pallas_call(
        _outproj_body,
        grid=(T // tm,),
        in_specs=[row(D), _const_spec((1, D)), _const_spec(wg.shape), _const_spec((1, 2 * D)),
                  row(om.shape[1]), slab, slab, slab,
                  row(LANES), row(LANES), row(LANES), _const_spec(ex.shape),
                  _const_spec(wom.shape), _const_spec(wod.shape), _const_spec(wout.shape),
                  _const_spec((1, D)), _const_spec(wrh.shape), _const_spec(wrl.shape),
                  _const_spec((1, LANES))],
        out_specs=[row(D), row(D), row(LANES)],
        out_shape=[jax.ShapeDtypeStruct((T, D), F32),
                   jax.ShapeDtypeStruct((T, D), BF16),
                   jax.ShapeDtypeStruct((T, LANES), F32)],
        compiler_params=_params(1),
        name="outproj_router",
    )(x2, g1, wg, bg, om, ods[0], ods[1], ods[2], lss[0], lss[1], lss[2], ex,
      wom, wod, wout, g2, wrh, wrl, br)


def _moe_dense_body(h2_ref, comb_ref, x1_ref, wg_ref, wu_ref, wd_ref, o_ref):
    e = pl.program_id(1)

    @pl.when(e == 0)
    def _():
        o_ref[...] = x1_ref[...]

    h = h2_ref[...]
    a = jnp.dot(h, wg_ref[...], preferred_element_type=F32)
    b = jnp.dot(h, wu_ref[...], preferred_element_type=F32)
    he = (a * _sigmoid(a) * b).astype(BF16)
    lane = lax.broadcasted_iota(jnp.int32, (1, LANES), 1)
    c = jnp.sum(jnp.where(lane == e, comb_ref[...], 0.0), axis=-1, keepdims=True)
    o_ref[...] += c * jnp.dot(he, wd_ref[...], preferred_element_type=F32)


def _moe_dense(h2, comb, x1, wg, wu, wd, tm):
    T, D = x1.shape
    E, _, F = wg.shape
    return pl.pallas_call(
        _moe_dense_body,
        grid=(T // tm, E),
        in_specs=[pl.BlockSpec((tm, D), lambda i, e: (i, 0)),
                  pl.BlockSpec((tm, LANES), lambda i, e: (i, 0)),
                  pl.BlockSpec((tm, D), lambda i, e: (i, 0)),
                  pl.BlockSpec((None, D, F), lambda i, e: (e, 0, 0)),
                  pl.BlockSpec((None, D, F), lambda i, e: (e, 0, 0)),
                  pl.BlockSpec((None, F, D), lambda i, e: (e, 0, 0))],
        out_specs=pl.BlockSpec((tm, D), lambda i, e: (i, 0)),
        out_shape=jax.ShapeDtypeStruct((T, D), F32),
        compiler_params=_params(2),
        name="moe_dense",
    )(h2, comb, x1, wg, wu, wd)


def _rot_half_cols(w):
    half = w.shape[-1] // 2
    return jnp.concatenate([-w[..., half:], w[..., :half]], axis=-1)


def _swap_halves(g):
    half = g.shape[-1] // 2
    return jnp.concatenate([g[..., half:], g[..., :half]], axis=-1)


def kernel(x, positions, norm_attn, w_in, b_gate, norm_ckv, w_ukv, q_norm_mla, k_norm_mla, q_norm_dil, k_norm_dil, w_o_mla, w_o_dil, w_out, norm_ffn, w_router_group, b_router_group, w_router_expert, b_router_expert, w_gate, w_up, w_down):
    B, S, D = x.shape
    T = B * S
    depth = norm_attn.shape[0]
    q_cols = MLA_HEADS * MLA_QK
    c0 = q_cols
    c1 = c0 + MLA_KV_RANK
    c2 = c1 + MLA_ROPE
    c3 = c2 + 3 * N_DIL_GROUPS * DIL_GROUP_COLS

    half = MLA_ROPE // 2
    freqs = ROPE_THETA ** (-jnp.arange(half, dtype=F32) / half)
    ang = positions.astype(F32).reshape(T, 1) * freqs
    cos, sin = jnp.cos(ang), jnp.sin(ang)
    cs = jnp.concatenate([jnp.ones((T, MLA_NOPE), F32), cos, cos, sin, sin], axis=-1)

    ex = jnp.asarray(np.kron(np.eye(LANES, DIL_HEADS, dtype=np.float32),
                             np.ones((1, DIL_HEAD_DIM), np.float32)), dtype=BF16)

    xcur = x.reshape(T, D)
    for l in range(depth):
        w = w_in[l]
        wq = w[:, :c0].reshape(D, MLA_HEADS, MLA_QK)
        wq = jnp.concatenate([wq, _rot_half_cols(wq[:, :, MLA_NOPE:])], axis=-1).reshape(D, MLA_HEADS * LANES)
        wkr = w[:, c1:c2]
        wc = jnp.concatenate([w[:, c0:c1], jnp.zeros((D, MLA_NOPE), F32), wkr, _rot_half_cols(wkr)], axis=-1)
        wd = w[:, c2:c3].reshape(D, 3, N_DIL_GROUPS, DIL_GROUP_COLS).transpose(0, 2, 1, 3).reshape(D, c3 - c2)
        wgate = w[:, c3:]

        wukv = w_ukv[l].reshape(MLA_KV_RANK, MLA_HEADS, MLA_NOPE + MLA_V)
        wk = jnp.concatenate([wukv[:, :, :MLA_NOPE], jnp.zeros((MLA_KV_RANK, MLA_HEADS, LANES - MLA_NOPE), F32)], axis=-1)
        wukv2 = jnp.concatenate([wk.reshape(MLA_KV_RANK, MLA_HEADS * LANES),
                                 wukv[:, :, MLA_NOPE:].reshape(MLA_KV_RANK, MLA_HEADS * MLA_V)], axis=-1)

        qn, kn = q_norm_mla[l], k_norm_mla[l]
        scale = MLA_QK ** -0.5
        gq = (jnp.concatenate([qn, _swap_halves(qn[MLA_NOPE:])]) * scale).reshape(1, LANES)
        gkn = jnp.concatenate([kn[:MLA_NOPE], jnp.zeros((LANES - MLA_NOPE,), F32)]).reshape(1, LANES)
        gkr = jnp.concatenate([jnp.zeros((MLA_NOPE,), F32), kn[MLA_NOPE:], _swap_halves(kn[MLA_NOPE:])]).reshape(1, LANES)

        q1, cc, d0, d12 = _inproj(xcur, norm_attn[l].reshape(1, D), wq.astype(BF16), wc.astype(BF16),
                                  wd.astype(BF16), tm=512)
        qf, kf, v = _mla_prep(q1, cc, cs, gq, gkn, gkr, norm_ckv[l].reshape(1, MLA_KV_RANK),
                              wukv2.astype(BF16), tm=512)
        o_mla = _mla_attn(qf, kf, v, B, S, tq=512)

        ods, lss = [], []
        for g in range(N_DIL_GROUPS):
            gq2 = (jnp.tile(q_norm_dil[l, g], 2) * DIL_HEAD_DIM ** -0.5).reshape(1, LANES)
            gk2 = jnp.tile(k_norm_dil[l, g], 2).reshape(1, LANES)
            if g == 0:
                o, lse = _dil_attn(d0, 0, gq2, gk2, B, S, g)
            else:
                o, lse = _dil_attn(d12, (g - 1) * 3, gq2, gk2, B, S, g)
            ods.append(o)
            lss.append(lse)

        pad_cols = lambda a: jnp.pad(a, ((0, 0), (0, LANES - a.shape[1])))
        wr = pad_cols(jnp.concatenate([w_router_expert[l], w_router_group[l]], axis=1))
        br = pad_cols(jnp.concatenate([b_router_expert[l], b_router_group[l]]).reshape(1, -1))
        wrh = wr.astype(BF16)
        wrl = (wr - wrh.astype(F32)).astype(BF16)
        x1, h2, comb = _outproj(
            xcur, norm_attn[l].reshape(1, D), wgate.astype(BF16), b_gate[l].reshape(1, 2 * D),
            o_mla, ods, lss, ex, w_o_mla[l].astype(BF16), w_o_dil[l].astype(BF16), w_out[l].astype(BF16),
            norm_ffn[l].reshape(1, D), wrh, wrl, br, tm=256)
        xcur = _moe_dense(h2, comb, x1, w_gate[l].astype(BF16), w_up[l].astype(BF16), w_down[l].astype(BF16),
                          tm=1024)
    return xcur.reshape(B, S, D)
```

```python
import functools
import math

import jax
import jax.numpy as jnp
import numpy as np
from jax import lax
from jax.experimental import pallas as pl
from jax.experimental.pallas import tpu as pltpu

F32 = jnp.float32
BF16 = jnp.bfloat16
EPS = 1e-6
NEG = -1e30
LOG2E = math.log2(math.e)

LANES = 128
VMEM_LIMIT = 56 * 1024 * 1024

MLA_HEADS = 8
MLA_NOPE = 64
MLA_ROPE = 32
MLA_QK = MLA_NOPE + MLA_ROPE
MLA_V = 64
MLA_KV_RANK = 256
ROPE_THETA = 10000.0
DIL_PATTERNS = ((128, 1), (512, 4), (2048, 16))
N_DIL_GROUPS = 3
DIL_HEADS = 8
DIL_HEAD_DIM = 64
DIL_GROUP_COLS = DIL_HEADS * DIL_HEAD_DIM
BAND = 128
DIL_LOOKAHEAD = 6
DIL_UNROLL = 3
N_EXPERT_GROUPS = 4
EXPERTS_PER_GROUP = 8
N_EXPERTS = N_EXPERT_GROUPS * EXPERTS_PER_GROUP
D_FF_EXPERT = 256


def _params(n_axes):
    return pltpu.CompilerParams(dimension_semantics=("arbitrary",) * n_axes,
                                vmem_limit_bytes=VMEM_LIMIT)


def _const_spec(shape):
    nd = len(shape)
    return pl.BlockSpec(shape, lambda *_: (0,) * nd)


def _rms(x, gain):
    return x * lax.rsqrt(jnp.mean(x * x, axis=-1, keepdims=True) + EPS) * gain


def _sigmoid(x):
    return 1.0 / (1.0 + jnp.exp(-x))


def _split_bf16(v):
    hi = v.astype(BF16)
    lo = (v - hi.astype(F32)).astype(BF16)
    return hi, lo


def _inproj_body(x_ref, g_ref, wq_ref, wc_ref, wd_ref, q_ref, c_ref, d0_ref, d12_ref):
    h = _rms(x_ref[...], g_ref[...]).astype(BF16)
    q_ref[...] = jnp.dot(h, wq_ref[...], preferred_element_type=F32).astype(BF16)
    c_ref[...] = jnp.dot(h, wc_ref[...], preferred_element_type=F32)
    d = jnp.dot(h, wd_ref[...], preferred_element_type=F32)
    n0 = d0_ref.shape[0]
    for j in range(n0):
        d0_ref[j] = d[:, j * LANES:(j + 1) * LANES].astype(BF16)
    for j in range(d12_ref.shape[0]):
        d12_ref[j] = d[:, (n0 + j) * LANES:(n0 + j + 1) * LANES]


def _inproj(x2, g, wq, wc, wd, tm):
    T, D = x2.shape
    nq, nc, nd = wq.shape[1], wc.shape[1], wd.shape[1]
    n0 = 3 * DIL_GROUP_COLS // LANES
    n12 = nd // LANES - n0
    return pl.pallas_call(
        _inproj_body,
        grid=(T // tm,),
        in_specs=[pl.BlockSpec((tm, D), lambda i: (i, 0)),
                  _const_spec((1, D)), _const_spec((D, nq)), _const_spec((D, nc)), _const_spec((D, nd))],
        out_specs=[pl.BlockSpec((tm, nq), lambda i: (i, 0)),
                   pl.BlockSpec((tm, nc), lambda i: (i, 0)),
                   pl.BlockSpec((n0, tm, LANES), lambda i: (0, i, 0)),
                   pl.BlockSpec((n12, tm, LANES), lambda i: (0, i, 0))],
        out_shape=[jax.ShapeDtypeStruct((T, nq), BF16),
                   jax.ShapeDtypeStruct((T, nc), F32),
                   jax.ShapeDtypeStruct((n0, T, LANES), BF16),
                   jax.ShapeDtypeStruct((n12, T, LANES), F32)],
        compiler_params=_params(1),
        name="inproj",
    )(x2, g, wq, wc, wd)


def _mla_prep_body(q_ref, c_ref, cs_ref, gq_ref, gkn_ref, gkr_ref, gc_ref, wukv_ref,
                   qf_ref, kf_ref, v_ref):
    lane = lax.broadcasted_iota(jnp.int32, (1, LANES), 1)
    in_qk = lane < MLA_QK
    mid = jnp.logical_and(lane >= MLA_NOPE, lane < MLA_QK)
    hi = lane >= MLA_QK
    cs = cs_ref[...]
    qmul = cs * gq_ref[...]
    for h in range(MLA_HEADS):
        sl = slice(h * LANES, (h + 1) * LANES)
        qh = q_ref[:, sl].astype(F32)
        ssq = jnp.sum(jnp.where(in_qk, qh * qh, 0.0), axis=-1, keepdims=True)
        r = lax.rsqrt(ssq * (1.0 / MLA_QK) + EPS)
        qf_ref[:, sl] = (qh * qmul * r).astype(BF16)

    ckv = c_ref[:, :MLA_KV_RANK]
    kr = c_ref[:, MLA_KV_RANK:]
    cn = _rms(ckv, gc_ref[...]).astype(BF16)
    kv = jnp.dot(cn, wukv_ref[...], preferred_element_type=F32)
    xr = kr * (cs * gkr_ref[...])
    rk2 = xr + jnp.where(mid, pltpu.roll(xr, 96, 1), jnp.where(hi, pltpu.roll(xr, 32, 1), 0.0))
    ssq_r = jnp.sum(jnp.where(mid, kr * kr, 0.0), axis=-1, keepdims=True)
    gkn = gkn_ref[...]
    for h in range(MLA_HEADS):
        sl = slice(h * LANES, (h + 1) * LANES)
        kn = kv[:, sl]
        ssq = jnp.sum(kn * kn, axis=-1, keepdims=True) + ssq_r
        r = lax.rsqrt(ssq * (1.0 / MLA_QK) + EPS)
        kf_ref[:, sl] = ((kn * gkn + rk2) * r).astype(BF16)
    v_ref[...] = kv[:, MLA_HEADS * LANES:].astype(BF16)


def _mla_prep(q1, c1, cs, gq, gkn, gkr, gc, wukv, tm):
    T = q1.shape[0]
    nq, nc, nkv = q1.shape[1], c1.shape[1], wukv.shape[1]
    nv = MLA_HEADS * MLA_V
    return pl.pallas_call(
        _mla_prep_body,
        grid=(T // tm,),
        in_specs=[pl.BlockSpec((tm, nq), lambda i: (i, 0)),
                  pl.BlockSpec((tm, nc), lambda i: (i, 0)),
                  pl.BlockSpec((tm, LANES), lambda i: (i, 0)),
                  _const_spec((1, LANES)), _const_spec((1, LANES)), _const_spec((1, LANES)),
                  _const_spec((1, MLA_KV_RANK)), _const_spec((MLA_KV_RANK, nkv))],
        out_specs=[pl.BlockSpec((tm, nq), lambda i: (i, 0)),
                   pl.BlockSpec((tm, nq), lambda i: (i, 0)),
                   pl.BlockSpec((tm, nv), lambda i: (i, 0))],
        out_shape=[jax.ShapeDtypeStruct((T, nq), BF16),
                   jax.ShapeDtypeStruct((T, nq), BF16),
                   jax.ShapeDtypeStruct((T, nv), BF16)],
        compiler_params=_params(1),
        name="mla_prep",
    )(q1, c1, cs, gq, gkn, gkr, gc, wukv)


def _mla_attn_body(q_ref, k_ref, v_ref, o_ref, *, tq):
    S = q_ref.shape[0]
    nq = S // tq
    row = lax.broadcasted_iota(jnp.int32, (tq, tq), 0)
    col = lax.broadcasted_iota(jnp.int32, (tq, tq), 1)
    causal = col <= row
    lane = lax.broadcasted_iota(jnp.int32, (1, LANES), 1)
    first = lane < MLA_V
    for qi in range(nq):
        rows = slice(qi * tq, (qi + 1) * tq)
        outs = []
        for hh in range(2):
            hs = slice(hh * LANES, (hh + 1) * LANES)
            q = q_ref[rows, hs]
            m = jnp.full((tq, 1), -jnp.inf, F32)
            l = jnp.zeros((tq, 1), F32)
            acc = jnp.zeros((tq, LANES), F32)
            for ki in range(qi + 1):
                krows = slice(ki * tq, (ki + 1) * tq)
                s = lax.dot_general(q, k_ref[krows, hs], (((1,), (1,)), ((), ())),
                                    preferred_element_type=F32)
                if ki == qi:
                    s = jnp.where(causal, s, NEG)
                m_new = jnp.maximum(m, jnp.max(s, axis=-1, keepdims=True))
                p = jnp.exp2(s - m_new)
                alpha = jnp.exp2(m - m_new)
                l = alpha * l + jnp.sum(p, axis=-1, keepdims=True)
                acc = alpha * acc + jnp.dot(p.astype(BF16), v_ref[krows, :], preferred_element_type=F32)
                m = m_new
            outs.append(acc * (1.0 / l))
        o_ref[rows, :] = jnp.where(first, outs[0], outs[1]).astype(BF16)


def _mla_attn(qf, kf, v, B, S, tq):
    nq = qf.shape[1]
    pairs = MLA_HEADS // 2
    q3 = qf.reshape(B, S, nq)
    k3 = kf.reshape(B, S, nq)
    v3 = v.reshape(B, S, MLA_HEADS * MLA_V)
    out = pl.pallas_call(
        functools.partial(_mla_attn_body, tq=tq),
        grid=(B, pairs),
        in_specs=[pl.BlockSpec((None, S, 2 * LANES), lambda b, p: (b, 0, p)),
                  pl.BlockSpec((None, S, 2 * LANES), lambda b, p: (b, 0, p)),
                  pl.BlockSpec((None, S, LANES), lambda b, p: (b, 0, p))],
        out_specs=pl.BlockSpec((None, S, LANES), lambda b, p: (b, 0, p)),
        out_shape=jax.ShapeDtypeStruct((B, S, MLA_HEADS * MLA_V), BF16),
        compiler_params=_params(2),
        name="mla_attn",
    )(q3, k3, v3)
    return out.reshape(B * S, MLA_HEADS * MLA_V)


def _dil_attn_body(q_ref, k_ref, v_ref, gq_ref, gk_ref, bd_ref, o_ref, m_ref, l_ref,
                   qn_scr, kn_scr, va_scr, vb_scr, bias_scr, *, dil, L, nsub):
    step = pl.program_id(1)
    lane = lax.broadcasted_iota(jnp.int32, (1, LANES), 1)
    lo = lane < DIL_HEAD_DIM
    inv_dh = 1.0 / DIL_HEAD_DIM
    bd = bd_ref[...]
    pairs = DIL_HEADS // 2

    def rows(j, start, n):
        if dil == 1:
            return pl.ds(start, n)
        return pl.ds(start * dil + step * nsub + j, n, stride=dil)

    def head_rsqrt(x):
        ssq = jnp.dot((x * x).astype(BF16), bd, preferred_element_type=F32)
        return lax.rsqrt(ssq * inv_dh + EPS)

    for j in range(nsub):
        for pb in range(pairs):
            sl = slice(pb * LANES, (pb + 1) * LANES)
            x = k_ref[pb, rows(j, 0, L), :].astype(F32)
            kn_scr[j, :, sl] = (x * head_rsqrt(x) * gk_ref[...]).astype(BF16)
            x = q_ref[pb, rows(j, 0, L), :].astype(F32)
            xn = x * head_rsqrt(x) * gq_ref[...]
            qn_scr[j, :, (2 * pb) * LANES:(2 * pb + 1) * LANES] = jnp.where(lo, xn, 0.0).astype(BF16)
            qn_scr[j, :, (2 * pb + 1) * LANES:(2 * pb + 2) * LANES] = jnp.where(lo, 0.0, xn).astype(BF16)
            v = v_ref[pb, rows(j, 0, L), :].astype(F32)
            va_scr[j, :, sl] = jnp.where(lo, v, 1.0).astype(BF16)
            vb_scr[j, :, sl] = jnp.where(lo, 1.0, v).astype(BF16)

    @pl.when(jnp.logical_and(pl.program_id(0) == 0, step == 0))
    def _():
        qi = lax.broadcasted_iota(jnp.int32, (BAND, 2 * BAND), 0)
        kj = lax.broadcasted_iota(jnp.int32, (BAND, 2 * BAND), 1)
        delta = qi + BAND - kj
        valid = jnp.logical_and(delta >= 0, delta <= BAND)
        dist = (delta * dil).astype(F32)
        for h in range(DIL_HEADS):
            slope = 2.0 ** (-8.0 * (h + 1) / DIL_HEADS) * LOG2E
            bias_scr[h] = jnp.where(valid, -slope * dist, NEG)

    def run_blocks(blocks):
        tasks = [(bi, pb, hh) for bi in range(len(blocks)) for pb in range(pairs) for hh in range(2)]
        scores = {}
        mrow = [jnp.zeros((BAND, LANES), F32) for _ in blocks]
        lrow = [jnp.ones((BAND, LANES), F32) for _ in blocks]
        acc_first = {}
        for i in range(len(tasks) + DIL_LOOKAHEAD):
            if i < len(tasks):
                bi, pb, hh = tasks[i]
                j, qs, ks, nk = blocks[bi]
                h = 2 * pb + hh
                qh = qn_scr[j, pl.ds(qs, BAND), h * LANES:(h + 1) * LANES]
                kp = kn_scr[j, pl.ds(ks, nk), pb * LANES:(pb + 1) * LANES]
                s = lax.dot_general(qh, kp, (((1,), (1,)), ((), ())), preferred_element_type=F32)
                scores[i] = s + bias_scr[h, :, 2 * BAND - nk:]
            if i >= DIL_LOOKAHEAD:
                t = i - DIL_LOOKAHEAD
                bi, pb, hh = tasks[t]
                j, qs, ks, nk = blocks[bi]
                s = scores.pop(t)
                m = jnp.max(s, axis=-1, keepdims=True)
                p = jnp.exp2(s - m).astype(BF16)
                vaug = (va_scr if hh == 0 else vb_scr)[j, pl.ds(ks, nk), pb * LANES:(pb + 1) * LANES]
                acc = jnp.dot(p, vaug, preferred_element_type=F32)
                pos = DIL_HEAD_DIM + pb if hh == 0 else pb
                mrow[bi] = jnp.where(lane == pos, m, mrow[bi])
                lrow[bi] = jnp.where(lane == pos, acc, lrow[bi])
                if hh == 0:
                    acc_first[bi, pb] = acc
                else:
                    o_ref[pb, rows(j, qs, BAND), :] = jnp.where(
                        lo, acc_first.pop((bi, pb)), acc).astype(o_ref.dtype)
                    if pb == pairs - 1:
                        m_ref[rows(j, qs, BAND), :] = mrow[bi]
                        l_ref[rows(j, qs, BAND), :] = lrow[bi]

    nb = L // BAND
    first = [(j, 0, 0, BAND) for j in range(nsub)]
    rest = nb - 1
    if rest <= DIL_UNROLL:
        run_blocks(first + [(0, b * BAND, (b - 1) * BAND, 2 * BAND) for b in range(1, nb)])
    else:
        run_blocks(first)

        def body(i, carry):
            blocks = []
            for t in range(DIL_UNROLL):
                blk = 1 + DIL_UNROLL * i + t
                blocks.append((0, pl.multiple_of(blk * BAND, BAND), pl.multiple_of((blk - 1) * BAND, BAND),
                               2 * BAND))
            run_blocks(blocks)
            return carry
        lax.fori_loop(0, rest // DIL_UNROLL, body, 0)
        done = 1 + (rest // DIL_UNROLL) * DIL_UNROLL
        if done < nb:
            run_blocks([(0, b * BAND, (b - 1) * BAND, 2 * BAND) for b in range(done, nb)])


def _dil_attn(slabs, first_slab_block, gq2, gk2, bd, B, S, g, nsub):
    window, dil = DIL_PATTERNS[g]
    assert window // dil == BAND
    L = S // dil
    assert L % BAND == 0 and dil % nsub == 0 and (nsub == 1 or L == BAND)
    T = B * S
    pairs = DIL_HEADS // 2

    def in_spec(which):
        return pl.BlockSpec((pairs, S, LANES), lambda b, r: (first_slab_block + which, b, 0))

    stat = pl.BlockSpec((S, LANES), lambda b, r: (b, 0))
    return pl.pallas_call(
        functools.partial(_dil_attn_body, dil=dil, L=L, nsub=nsub),
        grid=(B, dil // nsub),
        in_specs=[in_spec(0), in_spec(1), in_spec(2), _const_spec((1, LANES)), _const_spec((1, LANES)),
                  _const_spec((LANES, LANES))],
        out_specs=[pl.BlockSpec((pairs, S, LANES), lambda b, r: (0, b, 0)), stat, stat],
        out_shape=[jax.ShapeDtypeStruct((pairs, T, LANES), slabs.dtype),
                   jax.ShapeDtypeStruct((T, LANES), F32),
                   jax.ShapeDtypeStruct((T, LANES), F32)],
        scratch_shapes=[pltpu.VMEM((nsub, L, DIL_HEADS * LANES), BF16),
                        pltpu.VMEM((nsub, L, DIL_GROUP_COLS), BF16),
                        pltpu.VMEM((nsub, L, DIL_GROUP_COLS), BF16),
                        pltpu.VMEM((nsub, L, DIL_GROUP_COLS), BF16),
                        pltpu.VMEM((DIL_HEADS, BAND, 2 * BAND), F32)],
        compiler_params=_params(2),
        name=f"dil_attn_g{g}",
    )(slabs, slabs, slabs, gq2, gk2, bd)


def _outproj_body(x_ref, g1_ref, wg_ref, bg_ref, om_ref, od0_ref, od1_ref, od2_ref,
                  m0_ref, m1_ref, m2_ref, l0_ref, l1_ref, l2_ref, ex_ref, wom_ref, wod_ref, wout_ref,
                  g2_ref, wrh_ref, wrl_ref, br_ref, x1_ref, h2_ref, comb_ref):
    D = x_ref.shape[1]
    x = x_ref[...]
    h = _rms(x, g1_ref[...]).astype(BF16)
    gp = jnp.dot(h, wg_ref[...], preferred_element_type=F32) + bg_ref[...]
    gate_a = _sigmoid(gp[:, :D])
    gate_b = _sigmoid(gp[:, D:])

    m0, m1, m2 = m0_ref[...], m1_ref[...], m2_ref[...]
    mx = jnp.maximum(jnp.maximum(m0, m1), m2)
    e0, e1, e2 = jnp.exp2(m0 - mx), jnp.exp2(m1 - mx), jnp.exp2(m2 - mx)
    inv = 1.0 / (e0 * l0_ref[...] + e1 * l1_ref[...] + e2 * l2_ref[...])
    ex = ex_ref[...]

    def spread(w, od_ref):
        hi, lo = _split_bf16(w)
        wide = (jnp.dot(hi, ex, preferred_element_type=F32) + jnp.dot(lo, ex, preferred_element_type=F32))
        o = jnp.concatenate([od_ref[p] for p in range(od_ref.shape[0])], axis=-1)
        return wide * o.astype(F32)

    od = spread(e0 * inv, od0_ref) + spread(e1 * inv, od1_ref) + spread(e2 * inv, od2_ref)

    a = jnp.dot(om_ref[...], wom_ref[...], preferred_element_type=F32)
    b = jnp.dot(od.astype(BF16), wod_ref[...], preferred_element_type=F32)
    merged = gate_a * a + gate_b * b
    x1 = x + jnp.dot(merged.astype(BF16), wout_ref[...], preferred_element_type=F32)
    x1_ref[...] = x1
    h2 = _rms(x1, g2_ref[...])
    h2_ref[...] = h2.astype(BF16)

    lane = lax.broadcasted_iota(jnp.int32, (1, LANES), 1)
    lane_f = lane.astype(F32)
    big = float(LANES)
    h2h, h2l = _split_bf16(h2)
    wrh = wrh_ref[...]
    logits = (jnp.dot(h2h, wrh, preferred_element_type=F32)
              + jnp.dot(h2h, wrl_ref[...], preferred_element_type=F32)
              + jnp.dot(h2l, wrh, preferred_element_type=F32)) + br_ref[...]
    is_group = jnp.logical_and(lane >= N_EXPERTS, lane < N_EXPERTS + N_EXPERT_GROUPS)
    lg = jnp.where(is_group, logits, -jnp.inf)
    gmax = jnp.max(lg, axis=-1, keepdims=True)
    gi = jnp.min(jnp.where(lg == gmax, lane_f, big), axis=-1, keepdims=True) - float(N_EXPERTS)
    g_p = 1.0 / jnp.sum(jnp.exp(lg - gmax), axis=-1, keepdims=True)
    lo_e = gi * EXPERTS_PER_GROUP
    sel = jnp.logical_and(lane_f >= lo_e, lane_f < lo_e + EXPERTS_PER_GROUP)
    els = jnp.where(sel, logits, -jnp.inf)
    m1 = jnp.max(els, axis=-1, keepdims=True)
    i1 = jnp.min(jnp.where(els == m1, lane_f, big), axis=-1, keepdims=True)
    els2 = jnp.where(lane_f == i1, -jnp.inf, els)
    m2 = jnp.max(els2, axis=-1, keepdims=True)
    i2 = jnp.min(jnp.where(els2 == m2, lane_f, big), axis=-1, keepdims=True)
    t = jnp.exp(m2 - m1)
    w1 = g_p / (1.0 + t)
    w2 = w1 * t
    comb_ref[...] = jnp.where(lane_f == i1, w1, jnp.where(lane_f == i2, w2, 0.0))


def _outproj(x2, g1, wg, bg, om, ods, ms, ls, ex, wom, wod, wout, g2, wrh, wrl, br, tm):
    T, D = x2.shape
    row = lambda n: pl.BlockSpec((tm, n), lambda i: (i, 0))
    slab = pl.BlockSpec((DIL_HEADS // 2, tm, LANES), lambda i: (0, i, 0))
    return pl.pallas_call(
        _outproj_body,
        grid=(T // tm,),
        in_specs=[row(D), _const_spec((1, D)), _const_spec(wg.shape), _const_spec((1, 2 * D)),
                  row(om.shape[1]), slab, slab, slab,
                  row(LANES), row(LANES), row(LANES), row(LANES), row(LANES), row(LANES),
                  _const_spec(ex.shape),
                  _const_spec(wom.shape), _const_spec(wod.shape), _const_spec(wout.shape),
                  _const_spec((1, D)), _const_spec(wrh.shape), _const_spec(wrl.shape),
                  _const_spec((1, LANES))],
        out_specs=[row(D), row(D), row(LANES)],
        out_shape=[jax.ShapeDtypeStruct((T, D), F32),
                   jax.ShapeDtypeStruct((T, D), BF16),
                   jax.ShapeDtypeStruct((T, LANES), F32)],
        compiler_params=_params(1),
        name="outproj_router",
    )(x2, g1, wg, bg, om, ods[0], ods[1], ods[2], ms[0], ms[1], ms[2], ls[0], ls[1], ls[2], ex,
      wom, wod, wout, g2, wrh, wrl, br)


def _moe_dense_body(h2_ref, comb_ref, x1_ref, wg_ref, wu_ref, wd_ref, o_ref):
    e = pl.program_id(1)

    @pl.when(e == 0)
    def _():
        o_ref[...] = x1_ref[...]

    h = h2_ref[...]
    a = jnp.dot(h, wg_ref[...], preferred_element_type=F32)
    b = jnp.dot(h, wu_ref[...], preferred_element_type=F32)
    he = (a * _sigmoid(a) * b).astype(BF16)
    lane = lax.broadcasted_iota(jnp.int32, (1, LANES), 1)
    c = jnp.sum(jnp.where(lane == e, comb_ref[...], 0.0), axis=-1, keepdims=True)
    o_ref[...] += c * jnp.dot(he, wd_ref[...], preferred_element_type=F32)


def _moe_dense(h2, comb, x1, wg, wu, wd, tm):
    T, D = x1.shape
    E, _, F = wg.shape
    return pl.pallas_call(
        _moe_dense_body,
        grid=(T // tm, E),
        in_specs=[pl.BlockSpec((tm, D), lambda i, e: (i, 0)),
                  pl.BlockSpec((tm, LANES), lambda i, e: (i, 0)),
                  pl.BlockSpec((tm, D), lambda i, e: (i, 0)),
                  pl.BlockSpec((None, D, F), lambda i, e: (e, 0, 0)),
                  pl.BlockSpec((None, D, F), lambda i, e: (e, 0, 0)),
                  pl.BlockSpec((None, F, D), lambda i, e: (e, 0, 0))],
        out_specs=pl.BlockSpec((tm, D), lambda i, e: (i, 0)),
        out_shape=jax.ShapeDtypeStruct((T, D), F32),
        compiler_params=_params(2),
        name="moe_dense",
    )(h2, comb, x1, wg, wu, wd)


def _rot_half_cols(w):
    half = w.shape[-1] // 2
    return jnp.concatenate([-w[..., half:], w[..., :half]], axis=-1)


def _swap_halves(g):
    half = g.shape[-1] // 2
    return jnp.concatenate([g[..., half:], g[..., :half]], axis=-1)


def kernel(x, positions, norm_attn, w_in, b_gate, norm_ckv, w_ukv, q_norm_mla, k_norm_mla, q_norm_dil, k_norm_dil, w_o_mla, w_o_dil, w_out, norm_ffn, w_router_group, b_router_group, w_router_expert, b_router_expert, w_gate, w_up, w_down):
    B, S, D = x.shape
    T = B * S
    depth = norm_attn.shape[0]
    q_cols = MLA_HEADS * MLA_QK
    c0 = q_cols
    c1 = c0 + MLA_KV_RANK
    c2 = c1 + MLA_ROPE
    c3 = c2 + 3 * N_DIL_GROUPS * DIL_GROUP_COLS

    half = MLA_ROPE // 2
    freqs = ROPE_THETA ** (-jnp.arange(half, dtype=F32) / half)
    ang = positions.astype(F32).reshape(T, 1) * freqs
    cos, sin = jnp.cos(ang), jnp.sin(ang)
    cs = jnp.concatenate([jnp.ones((T, MLA_NOPE), F32), cos, cos, sin, sin], axis=-1)

    ex_np = np.zeros((LANES, DIL_GROUP_COLS), np.float32)
    for p in range(DIL_HEADS // 2):
        ex_np[DIL_HEAD_DIM + p, (2 * p) * DIL_HEAD_DIM:(2 * p + 1) * DIL_HEAD_DIM] = 1.0
        ex_np[p, (2 * p + 1) * DIL_HEAD_DIM:(2 * p + 2) * DIL_HEAD_DIM] = 1.0
    ex = jnp.asarray(ex_np, dtype=BF16)
    bd = jnp.asarray(np.kron(np.eye(2, dtype=np.float32), np.ones((DIL_HEAD_DIM, DIL_HEAD_DIM), np.float32)),
                     dtype=BF16)

    xcur = x.reshape(T, D)
    for l in range(depth):
        w = w_in[l]
        wq = w[:, :c0].reshape(D, MLA_HEADS, MLA_QK)
        wq = jnp.concatenate([wq, _rot_half_cols(wq[:, :, MLA_NOPE:])], axis=-1).reshape(D, MLA_HEADS * LANES)
        wkr = w[:, c1:c2]
        wc = jnp.concatenate([w[:, c0:c1], jnp.zeros((D, MLA_NOPE), F32), wkr, _rot_half_cols(wkr)], axis=-1)
        wd = w[:, c2:c3].reshape(D, 3, N_DIL_GROUPS, DIL_GROUP_COLS).transpose(0, 2, 1, 3).reshape(D, c3 - c2)
        wgate = w[:, c3:]

        wukv = w_ukv[l].reshape(MLA_KV_RANK, MLA_HEADS, MLA_NOPE + MLA_V)
        wk = jnp.concatenate([wukv[:, :, :MLA_NOPE], jnp.zeros((MLA_KV_RANK, MLA_HEADS, LANES - MLA_NOPE), F32)], axis=-1)
        wukv2 = jnp.concatenate([wk.reshape(MLA_KV_RANK, MLA_HEADS * LANES),
                                 wukv[:, :, MLA_NOPE:].reshape(MLA_KV_RANK, MLA_HEADS * MLA_V)], axis=-1)

        qn, kn = q_norm_mla[l], k_norm_mla[l]
        scale = MLA_QK ** -0.5 * LOG2E
        gq = (jnp.concatenate([qn, _swap_halves(qn[MLA_NOPE:])]) * scale).reshape(1, LANES)
        gkn = jnp.concatenate([kn[:MLA_NOPE], jnp.zeros((LANES - MLA_NOPE,), F32)]).reshape(1, LANES)
        gkr = jnp.concatenate([jnp.zeros((MLA_NOPE,), F32), kn[MLA_NOPE:], _swap_halves(kn[MLA_NOPE:])]).reshape(1, LANES)

        q1, cc, d0, d12 = _inproj(xcur, norm_attn[l].reshape(1, D), wq.astype(BF16), wc.astype(BF16),
                                  wd.astype(BF16), tm=512)
        qf, kf, v = _mla_prep(q1, cc, cs, gq, gkn, gkr, norm_ckv[l].reshape(1, MLA_KV_RANK),
                              wukv2.astype(BF16), tm=512)
        o_mla = _mla_attn(qf, kf, v, B, S, tq=512)

        ods, ms, ls = [], [], []
        for g in range(N_DIL_GROUPS):
            gq2 = (jnp.tile(q_norm_dil[l, g], 2) * (DIL_HEAD_DIM ** -0.5 * LOG2E)).reshape(1, LANES)
            gk2 = jnp.tile(k_norm_dil[l, g], 2).reshape(1, LANES)
            if g == 0:
                o, m, den = _dil_attn(d0, 0, gq2, gk2, bd, B, S, g, nsub=1)
            else:
                o, m, den = _dil_attn(d12, (g - 1) * 3, gq2, gk2, bd, B, S, g, nsub=1 if g == 1 else 4)
            ods.append(o)
            ms.append(m)
            ls.append(den)

        pad_cols = lambda a: jnp.pad(a, ((0, 0), (0, LANES - a.shape[1])))
        wr = pad_cols(jnp.concatenate([w_router_expert[l], w_router_group[l]], axis=1))
        br = pad_cols(jnp.concatenate([b_router_expert[l], b_router_group[l]]).reshape(1, -1))
        wrh = wr.astype(BF16)
        wrl = (wr - wrh.astype(F32)).astype(BF16)
        x1, h2, comb = _outproj(
            xcur, norm_attn[l].reshape(1, D), wgate.astype(BF16), b_gate[l].reshape(1, 2 * D),
            o_mla, ods, ms, ls, ex, w_o_mla[l].astype(BF16), w_o_dil[l].astype(BF16), w_out[l].astype(BF16),
            norm_ffn[l].reshape(1, D), wrh, wrl, br, tm=512)
        xcur = _moe_dense(h2, comb, x1, w_gate[l].astype(BF16), w_up[l].astype(BF16), w_down[l].astype(BF16),
                          tm=1024)
    return xcur.reshape(B, S, D)
```

```python
import functools
import math

import jax
import jax.numpy as jnp
import numpy as np
from jax import lax
from jax.experimental import pallas as pl
from jax.experimental.pallas import tpu as pltpu

F32 = jnp.float32
BF16 = jnp.bfloat16
EPS = 1e-6
NEG = -1e30
LOG2E = math.log2(math.e)

LANES = 128
VMEM_LIMIT = 56 * 1024 * 1024

MLA_HEADS = 8
MLA_NOPE = 64
MLA_ROPE = 32
MLA_QK = MLA_NOPE + MLA_ROPE
MLA_V = 64
MLA_KV_RANK = 256
ROPE_THETA = 10000.0
DIL_PATTERNS = ((128, 1), (512, 4), (2048, 16))
N_DIL_GROUPS = 3
DIL_HEADS = 8
DIL_HEAD_DIM = 64
DIL_GROUP_COLS = DIL_HEADS * DIL_HEAD_DIM
BAND = 128
DIL_LOOKAHEAD = 6
DIL_UNROLL = 3
N_EXPERT_GROUPS = 4
EXPERTS_PER_GROUP = 8
N_EXPERTS = N_EXPERT_GROUPS * EXPERTS_PER_GROUP
D_FF_EXPERT = 256
PAIRS_PER_GROUP = EXPERTS_PER_GROUP * (EXPERTS_PER_GROUP - 1) // 2
N_CLASSES = N_EXPERT_GROUPS * PAIRS_PER_GROUP
MOE_TILE = 128


def _class_expert_tables():
    ea, eb = [], []
    for g in range(N_EXPERT_GROUPS):
        for a in range(EXPERTS_PER_GROUP):
            for b in range(a + 1, EXPERTS_PER_GROUP):
                ea.append(g * EXPERTS_PER_GROUP + a)
                eb.append(g * EXPERTS_PER_GROUP + b)
    return np.asarray(ea, np.int32), np.asarray(eb, np.int32)


_CLASS_EXPERT_A, _CLASS_EXPERT_B = _class_expert_tables()


def _params(n_axes):
    return pltpu.CompilerParams(dimension_semantics=("arbitrary",) * n_axes,
                                vmem_limit_bytes=VMEM_LIMIT)


def _const_spec(shape):
    nd = len(shape)
    return pl.BlockSpec(shape, lambda *_: (0,) * nd)


def _rms(x, gain):
    return x * lax.rsqrt(jnp.mean(x * x, axis=-1, keepdims=True) + EPS) * gain


def _sigmoid(x):
    return 1.0 / (1.0 + jnp.exp(-x))


def _split_bf16(v):
    hi = v.astype(BF16)
    lo = (v - hi.astype(F32)).astype(BF16)
    return hi, lo


def _inproj_body(x_ref, g_ref, wq_ref, wc_ref, wd_ref, q_ref, c_ref, d0_ref, d12_ref):
    h = _rms(x_ref[...], g_ref[...]).astype(BF16)
    q_ref[...] = jnp.dot(h, wq_ref[...], preferred_element_type=F32).astype(BF16)
    c_ref[...] = jnp.dot(h, wc_ref[...], preferred_element_type=F32)
    d = jnp.dot(h, wd_ref[...], preferred_element_type=F32)
    n0 = d0_ref.shape[0]
    for j in range(n0):
        d0_ref[j] = d[:, j * LANES:(j + 1) * LANES].astype(BF16)
    for j in range(d12_ref.shape[0]):
        d12_ref[j] = d[:, (n0 + j) * LANES:(n0 + j + 1) * LANES]


def _inproj(x2, g, wq, wc, wd, tm):
    T, D = x2.shape
    nq, nc, nd = wq.shape[1], wc.shape[1], wd.shape[1]
    n0 = 3 * DIL_GROUP_COLS // LANES
    n12 = nd // LANES - n0
    return pl.pallas_call(
        _inproj_body,
        grid=(T // tm,),
        in_specs=[pl.BlockSpec((tm, D), lambda i: (i, 0)),
                  _const_spec((1, D)), _const_spec((D, nq)), _const_spec((D, nc)), _const_spec((D, nd))],
        out_specs=[pl.BlockSpec((tm, nq), lambda i: (i, 0)),
                   pl.BlockSpec((tm, nc), lambda i: (i, 0)),
                   pl.BlockSpec((n0, tm, LANES), lambda i: (0, i, 0)),
                   pl.BlockSpec((n12, tm, LANES), lambda i: (0, i, 0))],
        out_shape=[jax.ShapeDtypeStruct((T, nq), BF16),
                   jax.ShapeDtypeStruct((T, nc), F32),
                   jax.ShapeDtypeStruct((n0, T, LANES), BF16),
                   jax.ShapeDtypeStruct((n12, T, LANES), F32)],
        compiler_params=_params(1),
        name="inproj",
    )(x2, g, wq, wc, wd)


def _mla_prep_body(q_ref, c_ref, cs_ref, gq_ref, gkn_ref, gkr_ref, gc_ref, wukv_ref,
                   qf_ref, kf_ref, v_ref):
    lane = lax.broadcasted_iota(jnp.int32, (1, LANES), 1)
    in_qk = lane < MLA_QK
    mid = jnp.logical_and(lane >= MLA_NOPE, lane < MLA_QK)
    hi = lane >= MLA_QK
    cs = cs_ref[...]
    qmul = cs * gq_ref[...]
    for h in range(MLA_HEADS):
        sl = slice(h * LANES, (h + 1) * LANES)
        qh = q_ref[:, sl].astype(F32)
        ssq = jnp.sum(jnp.where(in_qk, qh * qh, 0.0), axis=-1, keepdims=True)
        r = lax.rsqrt(ssq * (1.0 / MLA_QK) + EPS)
        qf_ref[:, sl] = (qh * qmul * r).astype(BF16)

    ckv = c_ref[:, :MLA_KV_RANK]
    kr = c_ref[:, MLA_KV_RANK:]
    cn = _rms(ckv, gc_ref[...]).astype(BF16)
    kv = jnp.dot(cn, wukv_ref[...], preferred_element_type=F32)
    xr = kr * (cs * gkr_ref[...])
    rk2 = xr + jnp.where(mid, pltpu.roll(xr, 96, 1), jnp.where(hi, pltpu.roll(xr, 32, 1), 0.0))
    ssq_r = jnp.sum(jnp.where(mid, kr * kr, 0.0), axis=-1, keepdims=True)
    gkn = gkn_ref[...]
    for h in range(MLA_HEADS):
        sl = slice(h * LANES, (h + 1) * LANES)
        kn = kv[:, sl]
        ssq = jnp.sum(kn * kn, axis=-1, keepdims=True) + ssq_r
        r = lax.rsqrt(ssq * (1.0 / MLA_QK) + EPS)
        kf_ref[:, sl] = ((kn * gkn + rk2) * r).astype(BF16)
    v_ref[...] = kv[:, MLA_HEADS * LANES:].astype(BF16)


def _mla_prep(q1, c1, cs, gq, gkn, gkr, gc, wukv, tm):
    T = q1.shape[0]
    nq, nc, nkv = q1.shape[1], c1.shape[1], wukv.shape[1]
    nv = MLA_HEADS * MLA_V
    return pl.pallas_call(
        _mla_prep_body,
        grid=(T // tm,),
        in_specs=[pl.BlockSpec((tm, nq), lambda i: (i, 0)),
                  pl.BlockSpec((tm, nc), lambda i: (i, 0)),
                  pl.BlockSpec((tm, LANES), lambda i: (i, 0)),
                  _const_spec((1, LANES)), _const_spec((1, LANES)), _const_spec((1, LANES)),
                  _const_spec((1, MLA_KV_RANK)), _const_spec((MLA_KV_RANK, nkv))],
        out_specs=[pl.BlockSpec((tm, nq), lambda i: (i, 0)),
                   pl.BlockSpec((tm, nq), lambda i: (i, 0)),
                   pl.BlockSpec((tm, nv), lambda i: (i, 0))],
        out_shape=[jax.ShapeDtypeStruct((T, nq), BF16),
                   jax.ShapeDtypeStruct((T, nq), BF16),
                   jax.ShapeDtypeStruct((T, nv), BF16)],
        compiler_params=_params(1),
        name="mla_prep",
    )(q1, c1, cs, gq, gkn, gkr, gc, wukv)


def _mla_attn_body(q_ref, k_ref, v_ref, o_ref, *, tq):
    S = q_ref.shape[0]
    nq = S // tq
    row = lax.broadcasted_iota(jnp.int32, (tq, tq), 0)
    col = lax.broadcasted_iota(jnp.int32, (tq, tq), 1)
    causal = col <= row
    lane = lax.broadcasted_iota(jnp.int32, (1, LANES), 1)
    first = lane < MLA_V
    for qi in range(nq):
        rows = slice(qi * tq, (qi + 1) * tq)
        outs = []
        for hh in range(2):
            hs = slice(hh * LANES, (hh + 1) * LANES)
            q = q_ref[rows, hs]
            m = jnp.full((tq, 1), -jnp.inf, F32)
            l = jnp.zeros((tq, 1), F32)
            acc = jnp.zeros((tq, LANES), F32)
            for ki in range(qi + 1):
                krows = slice(ki * tq, (ki + 1) * tq)
                s = lax.dot_general(q, k_ref[krows, hs], (((1,), (1,)), ((), ())),
                                    preferred_element_type=F32)
                if ki == qi:
                    s = jnp.where(causal, s, NEG)
                m_new = jnp.maximum(m, jnp.max(s, axis=-1, keepdims=True))
                p = jnp.exp2(s - m_new)
                alpha = jnp.exp2(m - m_new)
                l = alpha * l + jnp.sum(p, axis=-1, keepdims=True)
                acc = alpha * acc + jnp.dot(p.astype(BF16), v_ref[krows, :], preferred_element_type=F32)
                m = m_new
            outs.append(acc * (1.0 / l))
        o_ref[rows, :] = jnp.where(first, outs[0], outs[1]).astype(BF16)


def _mla_attn(qf, kf, v, B, S, tq):
    nq = qf.shape[1]
    pairs = MLA_HEADS // 2
    q3 = qf.reshape(B, S, nq)
    k3 = kf.reshape(B, S, nq)
    v3 = v.reshape(B, S, MLA_HEADS * MLA_V)
    out = pl.pallas_call(
        functools.partial(_mla_attn_body, tq=tq),
        grid=(B, pairs),
        in_specs=[pl.BlockSpec((None, S, 2 * LANES), lambda b, p: (b, 0, p)),
                  pl.BlockSpec((None, S, 2 * LANES), lambda b, p: (b, 0, p)),
                  pl.BlockSpec((None, S, LANES), lambda b, p: (b, 0, p))],
        out_specs=pl.BlockSpec((None, S, LANES), lambda b, p: (b, 0, p)),
        out_shape=jax.ShapeDtypeStruct((B, S, MLA_HEADS * MLA_V), BF16),
        compiler_params=_params(2),
        name="mla_attn",
    )(q3, k3, v3)
    return out.reshape(B * S, MLA_HEADS * MLA_V)


def _dil_attn_body(q_ref, k_ref, v_ref, gq_ref, gk_ref, bd_ref, o_ref, m_ref, l_ref,
                   qn_scr, kn_scr, va_scr, vb_scr, bias_scr, *, dil, L, nsub):
    step = pl.program_id(1)
    lane = lax.broadcasted_iota(jnp.int32, (1, LANES), 1)
    lo = lane < DIL_HEAD_DIM
    inv_dh = 1.0 / DIL_HEAD_DIM
    bd = bd_ref[...]
    pairs = DIL_HEADS // 2

    def rows(j, start, n):
        if dil == 1:
            return pl.ds(start, n)
        return pl.ds(start * dil + step * nsub + j, n, stride=dil)

    def head_rsqrt(x):
        ssq = jnp.dot((x * x).astype(BF16), bd, preferred_element_type=F32)
        return lax.rsqrt(ssq * inv_dh + EPS)

    for j in range(nsub):
        for pb in range(pairs):
            sl = slice(pb * LANES, (pb + 1) * LANES)
            x = k_ref[pb, rows(j, 0, L), :].astype(F32)
            kn_scr[j, :, sl] = (x * head_rsqrt(x) * gk_ref[...]).astype(BF16)
            x = q_ref[pb, rows(j, 0, L), :].astype(F32)
            xn = x * head_rsqrt(x) * gq_ref[...]
            qn_scr[j, :, (2 * pb) * LANES:(2 * pb + 1) * LANES] = jnp.where(lo, xn, 0.0).astype(BF16)
            qn_scr[j, :, (2 * pb + 1) * LANES:(2 * pb + 2) * LANES] = jnp.where(lo, 0.0, xn).astype(BF16)
            v = v_ref[pb, rows(j, 0, L), :].astype(F32)
            va_scr[j, :, sl] = jnp.where(lo, v, 1.0).astype(BF16)
            vb_scr[j, :, sl] = jnp.where(lo, 1.0, v).astype(BF16)

    @pl.when(jnp.logical_and(pl.program_id(0) == 0, step == 0))
    def _():
        qi = lax.broadcasted_iota(jnp.int32, (BAND, 2 * BAND), 0)
        kj = lax.broadcasted_iota(jnp.int32, (BAND, 2 * BAND), 1)
        delta = qi + BAND - kj
        valid = jnp.logical_and(delta >= 0, delta <= BAND)
        dist = (delta * dil).astype(F32)
        for h in range(DIL_HEADS):
            slope = 2.0 ** (-8.0 * (h + 1) / DIL_HEADS) * LOG2E
            bias_scr[h] = jnp.where(valid, -slope * dist, NEG)

    def run_blocks(blocks):
        tasks = [(bi, pb, hh) for bi in range(len(blocks)) for pb in range(pairs) for hh in range(2)]
        scores = {}
        mrow = [jnp.zeros((BAND, LANES), F32) for _ in blocks]
        lrow = [jnp.ones((BAND, LANES), F32) for _ in blocks]
        acc_first = {}
        for i in range(len(tasks) + DIL_LOOKAHEAD):
            if i < len(tasks):
                bi, pb, hh = tasks[i]
                j, qs, ks, nk = blocks[bi]
                h = 2 * pb + hh
                qh = qn_scr[j, pl.ds(qs, BAND), h * LANES:(h + 1) * LANES]
                kp = kn_scr[j, pl.ds(ks, nk), pb * LANES:(pb + 1) * LANES]
                s = lax.dot_general(qh, kp, (((1,), (1,)), ((), ())), preferred_element_type=F32)
                scores[i] = s + bias_scr[h, :, 2 * BAND - nk:]
            if i >= DIL_LOOKAHEAD:
                t = i - DIL_LOOKAHEAD
                bi, pb, hh = tasks[t]
                j, qs, ks, nk = blocks[bi]
                s = scores.pop(t)
                m = jnp.max(s, axis=-1, keepdims=True)
                p = jnp.exp2(s - m).astype(BF16)
                vaug = (va_scr if hh == 0 else vb_scr)[j, pl.ds(ks, nk), pb * LANES:(pb + 1) * LANES]
                acc = jnp.dot(p, vaug, preferred_element_type=F32)
                pos = DIL_HEAD_DIM + pb if hh == 0 else pb
                mrow[bi] = jnp.where(lane == pos, m, mrow[bi])
                lrow[bi] = jnp.where(lane == pos, acc, lrow[bi])
                if hh == 0:
                    acc_first[bi, pb] = acc
                else:
                    o_ref[pb, rows(j, qs, BAND), :] = jnp.where(
                        lo, acc_first.pop((bi, pb)), acc).astype(o_ref.dtype)
                    if pb == pairs - 1:
                        m_ref[rows(j, qs, BAND), :] = mrow[bi]
                        l_ref[rows(j, qs, BAND), :] = lrow[bi]

    nb = L // BAND
    first = [(j, 0, 0, BAND) for j in range(nsub)]
    rest = nb - 1
    if rest <= DIL_UNROLL:
        run_blocks(first + [(0, b * BAND, (b - 1) * BAND, 2 * BAND) for b in range(1, nb)])
    else:
        run_blocks(first)

        def body(i, carry):
            blocks = []
            for t in range(DIL_UNROLL):
                blk = 1 + DIL_UNROLL * i + t
                blocks.append((0, pl.multiple_of(blk * BAND, BAND), pl.multiple_of((blk - 1) * BAND, BAND),
                               2 * BAND))
            run_blocks(blocks)
            return carry
        lax.fori_loop(0, rest // DIL_UNROLL, body, 0)
        done = 1 + (rest // DIL_UNROLL) * DIL_UNROLL
        if done < nb:
            run_blocks([(0, b * BAND, (b - 1) * BAND, 2 * BAND) for b in range(done, nb)])


def _dil_attn(slabs, first_slab_block, gq2, gk2, bd, B, S, g, nsub):
    window, dil = DIL_PATTERNS[g]
    assert window // dil == BAND
    L = S // dil
    assert L % BAND == 0 and dil % nsub == 0 and (nsub == 1 or L == BAND)
    T = B * S
    pairs = DIL_HEADS // 2

    def in_spec(which):
        return pl.BlockSpec((pairs, S, LANES), lambda b, r: (first_slab_block + which, b, 0))

    stat = pl.BlockSpec((S, LANES), lambda b, r: (b, 0))
    return pl.pallas_call(
        functools.partial(_dil_attn_body, dil=dil, L=L, nsub=nsub),
        grid=(B, dil // nsub),
        in_specs=[in_spec(0), in_spec(1), in_spec(2), _const_spec((1, LANES)), _const_spec((1, LANES)),
                  _const_spec((LANES, LANES))],
        out_specs=[pl.BlockSpec((pairs, S, LANES), lambda b, r: (0, b, 0)), stat, stat],
        out_shape=[jax.ShapeDtypeStruct((pairs, T, LANES), slabs.dtype),
                   jax.ShapeDtypeStruct((T, LANES), F32),
                   jax.ShapeDtypeStruct((T, LANES), F32)],
        scratch_shapes=[pltpu.VMEM((nsub, L, DIL_HEADS * LANES), BF16),
                        pltpu.VMEM((nsub, L, DIL_GROUP_COLS), BF16),
                        pltpu.VMEM((nsub, L, DIL_GROUP_COLS), BF16),
                        pltpu.VMEM((nsub, L, DIL_GROUP_COLS), BF16),
                        pltpu.VMEM((DIL_HEADS, BAND, 2 * BAND), F32)],
        compiler_params=_params(2),
        name=f"dil_attn_g{g}",
    )(slabs, slabs, slabs, gq2, gk2, bd)


def _outproj_body(x_ref, g1_ref, wg_ref, bg_ref, om_ref, od0_ref, od1_ref, od2_ref,
                  m0_ref, m1_ref, m2_ref, l0_ref, l1_ref, l2_ref, ex_ref, wom_ref, wod_ref, wout_ref,
                  g2_ref, wrh_ref, wrl_ref, br_ref, x1_ref, oh_ref, cnt_ref):
    D = x_ref.shape[1]
    x = x_ref[...]
    h = _rms(x, g1_ref[...]).astype(BF16)
    gp = jnp.dot(h, wg_ref[...], preferred_element_type=F32) + bg_ref[...]
    gate_a = _sigmoid(gp[:, :D])
    gate_b = _sigmoid(gp[:, D:])

    m0, m1, m2 = m0_ref[...], m1_ref[...], m2_ref[...]
    mx = jnp.maximum(jnp.maximum(m0, m1), m2)
    e0, e1, e2 = jnp.exp2(m0 - mx), jnp.exp2(m1 - mx), jnp.exp2(m2 - mx)
    inv = 1.0 / (e0 * l0_ref[...] + e1 * l1_ref[...] + e2 * l2_ref[...])
    ex = ex_ref[...]

    def spread(w, od_ref):
        hi, lo = _split_bf16(w)
        wide = (jnp.dot(hi, ex, preferred_element_type=F32) + jnp.dot(lo, ex, preferred_element_type=F32))
        o = jnp.concatenate([od_ref[p] for p in range(od_ref.shape[0])], axis=-1)
        return wide * o.astype(F32)

    od = spread(e0 * inv, od0_ref) + spread(e1 * inv, od1_ref) + spread(e2 * inv, od2_ref)

    a = jnp.dot(om_ref[...], wom_ref[...], preferred_element_type=F32)
    b = jnp.dot(od.astype(BF16), wod_ref[...], preferred_element_type=F32)
    merged = gate_a * a + gate_b * b
    x1 = x + jnp.dot(merged.astype(BF16), wout_ref[...], preferred_element_type=F32)
    x1_ref[:, :D] = x1
    h2 = _rms(x1, g2_ref[...])

    lane = lax.broadcasted_iota(jnp.int32, (1, LANES), 1)
    lane_f = lane.astype(F32)
    big = float(LANES)
    h2h, h2l = _split_bf16(h2)
    wrh = wrh_ref[...]
    logits = (jnp.dot(h2h, wrh, preferred_element_type=F32)
              + jnp.dot(h2h, wrl_ref[...], preferred_element_type=F32)
              + jnp.dot(h2l, wrh, preferred_element_type=F32)) + br_ref[...]
    is_group = jnp.logical_and(lane >= N_EXPERTS, lane < N_EXPERTS + N_EXPERT_GROUPS)
    lg = jnp.where(is_group, logits, -jnp.inf)
    gmax = jnp.max(lg, axis=-1, keepdims=True)
    gi = jnp.min(jnp.where(lg == gmax, lane_f, big), axis=-1, keepdims=True) - float(N_EXPERTS)
    g_p = 1.0 / jnp.sum(jnp.exp(lg - gmax), axis=-1, keepdims=True)
    lo_e = gi * EXPERTS_PER_GROUP
    sel = jnp.logical_and(lane_f >= lo_e, lane_f < lo_e + EXPERTS_PER_GROUP)
    els = jnp.where(sel, logits, -jnp.inf)
    m1 = jnp.max(els, axis=-1, keepdims=True)
    i1 = jnp.min(jnp.where(els == m1, lane_f, big), axis=-1, keepdims=True)
    els2 = jnp.where(lane_f == i1, -jnp.inf, els)
    m2 = jnp.max(els2, axis=-1, keepdims=True)
    i2 = jnp.min(jnp.where(els2 == m2, lane_f, big), axis=-1, keepdims=True)
    t = jnp.exp(m2 - m1)
    w1 = g_p / (1.0 + t)
    w2 = w1 * t
    swap = i2 < i1
    a = jnp.where(swap, i2, i1) - lo_e
    b = jnp.where(swap, i1, i2) - lo_e
    pair = a * (2.0 * EXPERTS_PER_GROUP - 1.0 - a) * 0.5 + (b - a - 1.0)
    cls = gi * float(PAIRS_PER_GROUP) + pair
    x1_ref[:, D:] = jnp.where(lane == 0, jnp.where(swap, w2, w1), jnp.where(lane == 1, jnp.where(swap, w1, w2), 0.0))
    onehot = jnp.where(lane_f == cls, 1.0, 0.0)
    oh_ref[...] = onehot.astype(BF16)

    @pl.when(pl.program_id(0) == 0)
    def _():
        cnt_ref[...] = jnp.zeros_like(cnt_ref)

    cnt_ref[0:1, :] += jnp.sum(onehot, axis=0, keepdims=True)


def _outproj(x2, g1, wg, bg, om, ods, ms, ls, ex, wom, wod, wout, g2, wrh, wrl, br, tm):
    T, D = x2.shape
    row = lambda n: pl.BlockSpec((tm, n), lambda i: (i, 0))
    slab = pl.BlockSpec((DIL_HEADS // 2, tm, LANES), lambda i: (0, i, 0))
    return pl.pallas_call(
        _outproj_body,
        grid=(T // tm,),
        in_specs=[row(D), _const_spec((1, D)), _const_spec(wg.shape), _const_spec((1, 2 * D)),
                  row(om.shape[1]), slab, slab, slab,
                  row(LANES), row(LANES), row(LANES), row(LANES), row(LANES), row(LANES),
                  _const_spec(ex.shape),
                  _const_spec(wom.shape), _const_spec(wod.shape), _const_spec(wout.shape),
                  _const_spec((1, D)), _const_spec(wrh.shape), _const_spec(wrl.shape),
                  _const_spec((1, LANES))],
        out_specs=[row(D + LANES), row(LANES), _const_spec((8, LANES))],
        out_shape=[jax.ShapeDtypeStruct((T, D + LANES), F32),
                   jax.ShapeDtypeStruct((T, LANES), BF16),
                   jax.ShapeDtypeStruct((8, LANES), F32)],
        compiler_params=_params(1),
        name="outproj_router",
    )(x2, g1, wg, bg, om, ods[0], ods[1], ods[2], ms[0], ms[1], ms[2], ls[0], ls[1], ls[2], ex,
      wom, wod, wout, g2, wrh, wrl, br)


def _positions_body(oh_ref, offs_ref, ltri_ref, pos_ref, carry_scr):
    @pl.when(pl.program_id(0) == 0)
    def _():
        carry_scr[...] = jnp.zeros_like(carry_scr)

    oh = oh_ref[...]
    before = jnp.dot(ltri_ref[...], oh, preferred_element_type=F32) + carry_scr[0:1, :]
    val = jnp.where(oh > 0, before + offs_ref[...], 0.0)
    hi = jnp.floor(val * (1.0 / 256.0))
    lo = val - hi * 256.0
    ones = jnp.ones((8, LANES), BF16)
    nt = (((1,), (1,)), ((), ()))
    r = (256.0 * lax.dot_general(ones, hi.astype(BF16), nt, preferred_element_type=F32)
         + lax.dot_general(ones, lo.astype(BF16), nt, preferred_element_type=F32))
    pos_ref[...] = r.astype(jnp.int32)
    carry_scr[...] += jnp.dot(jnp.ones((8, oh.shape[0]), BF16), oh, preferred_element_type=F32)


def _positions(onehot, offs, chunk):
    T = onehot.shape[0]
    ltri = jnp.asarray(np.tril(np.ones((chunk, chunk), np.float32), -1), dtype=BF16)
    pos = pl.pallas_call(
        _positions_body,
        grid=(T // chunk,),
        in_specs=[pl.BlockSpec((chunk, LANES), lambda i: (i, 0)), _const_spec((1, LANES)),
                  _const_spec((chunk, chunk))],
        out_specs=pl.BlockSpec((None, 8, chunk), lambda i: (i, 0, 0)),
        out_shape=jax.ShapeDtypeStruct((T // chunk, 8, chunk), jnp.int32),
        scratch_shapes=[pltpu.VMEM((8, LANES), F32)],
        compiler_params=_params(1),
        name="moe_positions",
    )(onehot, offs, ltri)
    return pos[:, 0, :].reshape(T)


def _permute_rows_body(pos_ref, src_ref, *rest, tm, chunk, gather):
    dst_ref, sems = rest[-2:]
    base = pl.program_id(0) * tm

    def row_copy(r, slot):
        p = pos_ref[0, r]
        t = base + r
        s_row, d_row = (p, t) if gather else (t, p)
        return pltpu.make_async_copy(src_ref.at[pl.ds(s_row, 1), :], dst_ref.at[pl.ds(d_row, 1), :],
                                     sems.at[slot])

    def start(r, carry, slot):
        row_copy(r, slot).start()
        return carry

    def wait(r, carry, slot):
        row_copy(r, slot).wait()
        return carry

    nchunk = tm // chunk
    for c in range(nchunk + 1):
        if c < nchunk:
            lax.fori_loop(c * chunk, (c + 1) * chunk, functools.partial(start, slot=c % 2), 0, unroll=8)
        if c >= 1:
            lax.fori_loop((c - 1) * chunk, c * chunk, functools.partial(wait, slot=(c - 1) % 2), 0, unroll=8)


def _permute_rows(pos, src, dst_rows, gather, tm=2048, chunk=256):
    T = pos.shape[0]
    pos3 = pos.reshape(T // tm, 1, tm)
    width = src.shape[1]
    args = [pos3, src]
    in_specs = [pl.BlockSpec((None, 1, tm), lambda i: (i, 0, 0), memory_space=pltpu.SMEM),
                pl.BlockSpec(memory_space=pl.ANY)]
    aliases = {}
    if not gather:
        args.append(jnp.zeros((dst_rows, width), src.dtype))
        in_specs.append(pl.BlockSpec(memory_space=pl.ANY))
        aliases = {2: 0}
    return pl.pallas_call(
        functools.partial(_permute_rows_body, tm=tm, chunk=chunk, gather=gather),
        grid=(T // tm,),
        in_specs=in_specs,
        out_specs=pl.BlockSpec(memory_space=pl.ANY),
        out_shape=jax.ShapeDtypeStruct((dst_rows, width), src.dtype),
        scratch_shapes=[pltpu.SemaphoreType.DMA((2,))],
        input_output_aliases=aliases,
        compiler_params=_params(1),
        name="moe_gather_rows" if gather else "moe_dispatch_rows",
    )(*args)


def _pair_experts_body(ea_ref, eb_ref, blk_ref, nused_ref, xs_ref, g2_ref, wga_ref, wgb_ref, wua_ref, wub_ref,
                       wda_ref, wdb_ref, ys_ref):
    del ea_ref, eb_ref, blk_ref
    D = ys_ref.shape[1]

    @pl.when(pl.program_id(0) < nused_ref[0])
    def _():
        x1 = xs_ref[:, :D]
        w_a = xs_ref[:, D:D + 1]
        w_b = xs_ref[:, D + 1:D + 2]
        h = _rms(x1, g2_ref[...]).astype(BF16)

        def ffn(wg_ref, wu_ref, w):
            a = jnp.dot(h, wg_ref[...], preferred_element_type=F32)
            b = jnp.dot(h, wu_ref[...], preferred_element_type=F32)
            return (a * _sigmoid(a) * b * w).astype(BF16)

        y = (jnp.dot(ffn(wga_ref, wua_ref, w_a), wda_ref[...], preferred_element_type=F32)
             + jnp.dot(ffn(wgb_ref, wub_ref, w_b), wdb_ref[...], preferred_element_type=F32))
        ys_ref[...] = x1 + y

    @pl.when(pl.program_id(0) >= nused_ref[0])
    def _():
        ys_ref[...] = jnp.zeros_like(ys_ref)


def _pair_experts(tile_ea, tile_eb, tile_blk, n_used, xs, g2, wg, wu, wd, tm):
    rows, width = xs.shape
    E, D, F = wg.shape
    nt = rows // tm
    by_a = lambda i, ea, eb, blk, nu: (ea[i], 0, 0)
    by_b = lambda i, ea, eb, blk, nu: (eb[i], 0, 0)
    tile = lambda i, ea, eb, blk, nu: (blk[i], 0)
    grid_spec = pltpu.PrefetchScalarGridSpec(
        num_scalar_prefetch=4,
        grid=(nt,),
        in_specs=[pl.BlockSpec((tm, width), tile),
                  pl.BlockSpec((1, D), lambda i, ea, eb, blk, nu: (0, 0)),
                  pl.BlockSpec((None, D, F), by_a), pl.BlockSpec((None, D, F), by_b),
                  pl.BlockSpec((None, D, F), by_a), pl.BlockSpec((None, D, F), by_b),
                  pl.BlockSpec((None, F, D), by_a), pl.BlockSpec((None, F, D), by_b)],
        out_specs=pl.BlockSpec((tm, D), lambda i, ea, eb, blk, nu: (i, 0)),
    )
    return pl.pallas_call(
        _pair_experts_body,
        grid_spec=grid_spec,
        out_shape=jax.ShapeDtypeStruct((rows, D), F32),
        compiler_params=_params(1),
        name="moe_pair_experts",
    )(tile_ea, tile_eb, tile_blk, n_used, xs, g2, wg, wg, wu, wu, wd, wd)


def _rot_half_cols(w):
    half = w.shape[-1] // 2
    return jnp.concatenate([-w[..., half:], w[..., :half]], axis=-1)


def _swap_halves(g):
    half = g.shape[-1] // 2
    return jnp.concatenate([g[..., half:], g[..., :half]], axis=-1)


def kernel(x, positions, norm_attn, w_in, b_gate, norm_ckv, w_ukv, q_norm_mla, k_norm_mla, q_norm_dil, k_norm_dil, w_o_mla, w_o_dil, w_out, norm_ffn, w_router_group, b_router_group, w_router_expert, b_router_expert, w_gate, w_up, w_down):
    B, S, D = x.shape
    T = B * S
    depth = norm_attn.shape[0]
    q_cols = MLA_HEADS * MLA_QK
    c0 = q_cols
    c1 = c0 + MLA_KV_RANK
    c2 = c1 + MLA_ROPE
    c3 = c2 + 3 * N_DIL_GROUPS * DIL_GROUP_COLS

    half = MLA_ROPE // 2
    freqs = ROPE_THETA ** (-jnp.arange(half, dtype=F32) / half)
    ang = positions.astype(F32).reshape(T, 1) * freqs
    cos, sin = jnp.cos(ang), jnp.sin(ang)
    cs = jnp.concatenate([jnp.ones((T, MLA_NOPE), F32), cos, cos, sin, sin], axis=-1)

    ex_np = np.zeros((LANES, DIL_GROUP_COLS), np.float32)
    for p in range(DIL_HEADS // 2):
        ex_np[DIL_HEAD_DIM + p, (2 * p) * DIL_HEAD_DIM:(2 * p + 1) * DIL_HEAD_DIM] = 1.0
        ex_np[p, (2 * p + 1) * DIL_HEAD_DIM:(2 * p + 2) * DIL_HEAD_DIM] = 1.0
    ex = jnp.asarray(ex_np, dtype=BF16)
    bd = jnp.asarray(np.kron(np.eye(2, dtype=np.float32), np.ones((DIL_HEAD_DIM, DIL_HEAD_DIM), np.float32)),
                     dtype=BF16)

    xcur = x.reshape(T, D)
    for l in range(depth):
        w = w_in[l]
        wq = w[:, :c0].reshape(D, MLA_HEADS, MLA_QK)
        wq = jnp.concatenate([wq, _rot_half_cols(wq[:, :, MLA_NOPE:])], axis=-1).reshape(D, MLA_HEADS * LANES)
        wkr = w[:, c1:c2]
        wc = jnp.concatenate([w[:, c0:c1], jnp.zeros((D, MLA_NOPE), F32), wkr, _rot_half_cols(wkr)], axis=-1)
        wd = w[:, c2:c3].reshape(D, 3, N_DIL_GROUPS, DIL_GROUP_COLS).transpose(0, 2, 1, 3).reshape(D, c3 - c2)
        wgate = w[:, c3:]

        wukv = w_ukv[l].reshape(MLA_KV_RANK, MLA_HEADS, MLA_NOPE + MLA_V)
        wk = jnp.concatenate([wukv[:, :, :MLA_NOPE], jnp.zeros((MLA_KV_RANK, MLA_HEADS, LANES - MLA_NOPE), F32)], axis=-1)
        wukv2 = jnp.concatenate([wk.reshape(MLA_KV_RANK, MLA_HEADS * LANES),
                                 wukv[:, :, MLA_NOPE:].reshape(MLA_KV_RANK, MLA_HEADS * MLA_V)], axis=-1)

        qn, kn = q_norm_mla[l], k_norm_mla[l]
        scale = MLA_QK ** -0.5 * LOG2E
        gq = (jnp.concatenate([qn, _swap_halves(qn[MLA_NOPE:])]) * scale).reshape(1, LANES)
        gkn = jnp.concatenate([kn[:MLA_NOPE], jnp.zeros((LANES - MLA_NOPE,), F32)]).reshape(1, LANES)
        gkr = jnp.concatenate([jnp.zeros((MLA_NOPE,), F32), kn[MLA_NOPE:], _swap_halves(kn[MLA_NOPE:])]).reshape(1, LANES)

        q1, cc, d0, d12 = _inproj(xcur, norm_attn[l].reshape(1, D), wq.astype(BF16), wc.astype(BF16),
                                  wd.astype(BF16), tm=512)
        qf, kf, v = _mla_prep(q1, cc, cs, gq, gkn, gkr, norm_ckv[l].reshape(1, MLA_KV_RANK),
                              wukv2.astype(BF16), tm=512)
        o_mla = _mla_attn(qf, kf, v, B, S, tq=512)

        ods, ms, ls = [], [], []
        for g in range(N_DIL_GROUPS):
            gq2 = (jnp.tile(q_norm_dil[l, g], 2) * (DIL_HEAD_DIM ** -0.5 * LOG2E)).reshape(1, LANES)
            gk2 = jnp.tile(k_norm_dil[l, g], 2).reshape(1, LANES)
            if g == 0:
                o, m, den = _dil_attn(d0, 0, gq2, gk2, bd, B, S, g, nsub=1)
            else:
                o, m, den = _dil_attn(d12, (g - 1) * 3, gq2, gk2, bd, B, S, g, nsub=1 if g == 1 else 4)
            ods.append(o)
            ms.append(m)
            ls.append(den)

        pad_cols = lambda a: jnp.pad(a, ((0, 0), (0, LANES - a.shape[1])))
        wr = pad_cols(jnp.concatenate([w_router_expert[l], w_router_group[l]], axis=1))
        br = pad_cols(jnp.concatenate([b_router_expert[l], b_router_group[l]]).reshape(1, -1))
        wrh = wr.astype(BF16)
        wrl = (wr - wrh.astype(F32)).astype(BF16)
        x1w, onehot, cnt = _outproj(
            xcur, norm_attn[l].reshape(1, D), wgate.astype(BF16), b_gate[l].reshape(1, 2 * D),
            o_mla, ods, ms, ls, ex, w_o_mla[l].astype(BF16), w_o_dil[l].astype(BF16), w_out[l].astype(BF16),
            norm_ffn[l].reshape(1, D), wrh, wrl, br, tm=512)

        counts = cnt[0].astype(jnp.int32)
        tiles_per = (counts + MOE_TILE - 1) // MOE_TILE
        cum = jnp.cumsum(tiles_per)
        offs = ((cum - tiles_per) * MOE_TILE).astype(F32).reshape(1, LANES)
        n_tiles = T // MOE_TILE + N_CLASSES
        blk = jnp.minimum(jnp.arange(n_tiles, dtype=jnp.int32), cum[-1] - 1)
        tcls = jnp.minimum(jnp.searchsorted(cum, blk, side="right"), N_CLASSES - 1)
        tile_ea = jnp.asarray(_CLASS_EXPERT_A)[tcls]
        tile_eb = jnp.asarray(_CLASS_EXPERT_B)[tcls]

        pos = _positions(onehot, offs, chunk=512)
        xs = _permute_rows(pos, x1w, n_tiles * MOE_TILE, gather=False)
        ys = _pair_experts(tile_ea, tile_eb, blk, cum[-1:].astype(jnp.int32), xs, norm_ffn[l].reshape(1, D),
                           w_gate[l].astype(BF16), w_up[l].astype(BF16), w_down[l].astype(BF16), MOE_TILE)
        xcur = _permute_rows(pos, ys, T, gather=True)
    return xcur.reshape(B, S, D)
```

```python
import functools
import math

import jax
import jax.numpy as jnp
import numpy as np
from jax import lax
from jax.experimental import pallas as pl
from jax.experimental.pallas import tpu as pltpu

F32 = jnp.float32
BF16 = jnp.bfloat16
EPS = 1e-6
NEG = -1e30
LOG2E = math.log2(math.e)

LANES = 128
VMEM_LIMIT = 56 * 1024 * 1024

MLA_HEADS = 8
MLA_NOPE = 64
MLA_ROPE = 32
MLA_QK = MLA_NOPE + MLA_ROPE
MLA_V = 64
MLA_KV_RANK = 256
ROPE_THETA = 10000.0
DIL_PATTERNS = ((128, 1), (512, 4), (2048, 16))
N_DIL_GROUPS = 3
DIL_HEADS = 8
DIL_HEAD_DIM = 64
DIL_GROUP_COLS = DIL_HEADS * DIL_HEAD_DIM
BAND = 128
DIL_LOOKAHEAD = 6
DIL_UNROLL = 3
N_EXPERT_GROUPS = 4
EXPERTS_PER_GROUP = 8
N_EXPERTS = N_EXPERT_GROUPS * EXPERTS_PER_GROUP
D_FF_EXPERT = 256
PAIRS_PER_GROUP = EXPERTS_PER_GROUP * (EXPERTS_PER_GROUP - 1) // 2
N_CLASSES = N_EXPERT_GROUPS * PAIRS_PER_GROUP
MOE_TILE = 128


def _class_expert_tables():
    ea, eb = [], []
    for g in range(N_EXPERT_GROUPS):
        for a in range(EXPERTS_PER_GROUP):
            for b in range(a + 1, EXPERTS_PER_GROUP):
                ea.append(g * EXPERTS_PER_GROUP + a)
                eb.append(g * EXPERTS_PER_GROUP + b)
    return np.asarray(ea, np.int32), np.asarray(eb, np.int32)


_CLASS_EXPERT_A, _CLASS_EXPERT_B = _class_expert_tables()


def _params(n_axes):
    return pltpu.CompilerParams(dimension_semantics=("arbitrary",) * n_axes,
                                vmem_limit_bytes=VMEM_LIMIT)


def _const_spec(shape):
    nd = len(shape)
    return pl.BlockSpec(shape, lambda *_: (0,) * nd)


def _rms(x, gain):
    return x * lax.rsqrt(jnp.mean(x * x, axis=-1, keepdims=True) + EPS) * gain


def _sigmoid(x):
    return 1.0 / (1.0 + jnp.exp(-x))


def _split_bf16(v):
    hi = v.astype(BF16)
    lo = (v - hi.astype(F32)).astype(BF16)
    return hi, lo


def _inproj_body(x_ref, g_ref, wq_ref, wc_ref, wd_ref, q_ref, c_ref, d0_ref, d12_ref):
    h = _rms(x_ref[...], g_ref[...]).astype(BF16)
    q_ref[...] = jnp.dot(h, wq_ref[...], preferred_element_type=F32).astype(BF16)
    c_ref[...] = jnp.dot(h, wc_ref[...], preferred_element_type=F32)
    d = jnp.dot(h, wd_ref[...], preferred_element_type=F32)
    n0 = d0_ref.shape[0]
    for j in range(n0):
        d0_ref[j] = d[:, j * LANES:(j + 1) * LANES].astype(BF16)
    for j in range(d12_ref.shape[0]):
        d12_ref[j] = d[:, (n0 + j) * LANES:(n0 + j + 1) * LANES]


def _inproj(x2, g, wq, wc, wd, tm):
    T, D = x2.shape
    nq, nc, nd = wq.shape[1], wc.shape[1], wd.shape[1]
    n0 = 3 * DIL_GROUP_COLS // LANES
    n12 = nd // LANES - n0
    return pl.pallas_call(
        _inproj_body,
        grid=(T // tm,),
        in_specs=[pl.BlockSpec((tm, D), lambda i: (i, 0)),
                  _const_spec((1, D)), _const_spec((D, nq)), _const_spec((D, nc)), _const_spec((D, nd))],
        out_specs=[pl.BlockSpec((tm, nq), lambda i: (i, 0)),
                   pl.BlockSpec((tm, nc), lambda i: (i, 0)),
                   pl.BlockSpec((n0, tm, LANES), lambda i: (0, i, 0)),
                   pl.BlockSpec((n12, tm, LANES), lambda i: (0, i, 0))],
        out_shape=[jax.ShapeDtypeStruct((T, nq), BF16),
                   jax.ShapeDtypeStruct((T, nc), F32),
                   jax.ShapeDtypeStruct((n0, T, LANES), BF16),
                   jax.ShapeDtypeStruct((n12, T, LANES), F32)],
        compiler_params=_params(1),
        name="inproj",
    )(x2, g, wq, wc, wd)


def _mla_prep_body(q_ref, c_ref, cs_ref, gq_ref, gkn_ref, gkr_ref, gc_ref, wukv_ref,
                   qf_ref, kf_ref, v_ref):
    lane = lax.broadcasted_iota(jnp.int32, (1, LANES), 1)
    in_qk = lane < MLA_QK
    mid = jnp.logical_and(lane >= MLA_NOPE, lane < MLA_QK)
    hi = lane >= MLA_QK
    cs = cs_ref[...]
    qmul = cs * gq_ref[...]
    for h in range(MLA_HEADS):
        sl = slice(h * LANES, (h + 1) * LANES)
        qh = q_ref[:, sl].astype(F32)
        ssq = jnp.sum(jnp.where(in_qk, qh * qh, 0.0), axis=-1, keepdims=True)
        r = lax.rsqrt(ssq * (1.0 / MLA_QK) + EPS)
        qf_ref[:, sl] = (qh * qmul * r).astype(BF16)

    ckv = c_ref[:, :MLA_KV_RANK]
    kr = c_ref[:, MLA_KV_RANK:]
    cn = _rms(ckv, gc_ref[...]).astype(BF16)
    kv = jnp.dot(cn, wukv_ref[...], preferred_element_type=F32)
    xr = kr * (cs * gkr_ref[...])
    rk2 = xr + jnp.where(mid, pltpu.roll(xr, 96, 1), jnp.where(hi, pltpu.roll(xr, 32, 1), 0.0))
    ssq_r = jnp.sum(jnp.where(mid, kr * kr, 0.0), axis=-1, keepdims=True)
    gkn = gkn_ref[...]
    for h in range(MLA_HEADS):
        sl = slice(h * LANES, (h + 1) * LANES)
        kn = kv[:, sl]
        ssq = jnp.sum(kn * kn, axis=-1, keepdims=True) + ssq_r
        r = lax.rsqrt(ssq * (1.0 / MLA_QK) + EPS)
        kf_ref[:, sl] = ((kn * gkn + rk2) * r).astype(BF16)
    v_ref[...] = kv[:, MLA_HEADS * LANES:].astype(BF16)


def _mla_prep(q1, c1, cs, gq, gkn, gkr, gc, wukv, tm):
    T = q1.shape[0]
    nq, nc, nkv = q1.shape[1], c1.shape[1], wukv.shape[1]
    nv = MLA_HEADS * MLA_V
    return pl.pallas_call(
        _mla_prep_body,
        grid=(T // tm,),
        in_specs=[pl.BlockSpec((tm, nq), lambda i: (i, 0)),
                  pl.BlockSpec((tm, nc), lambda i: (i, 0)),
                  pl.BlockSpec((tm, LANES), lambda i: (i, 0)),
                  _const_spec((1, LANES)), _const_spec((1, LANES)), _const_spec((1, LANES)),
                  _const_spec((1, MLA_KV_RANK)), _const_spec((MLA_KV_RANK, nkv))],
        out_specs=[pl.BlockSpec((tm, nq), lambda i: (i, 0)),
                   pl.BlockSpec((tm, nq), lambda i: (i, 0)),
                   pl.BlockSpec((tm, nv), lambda i: (i, 0))],
        out_shape=[jax.ShapeDtypeStruct((T, nq), BF16),
                   jax.ShapeDtypeStruct((T, nq), BF16),
                   jax.ShapeDtypeStruct((T, nv), BF16)],
        compiler_params=_params(1),
        name="mla_prep",
    )(q1, c1, cs, gq, gkn, gkr, gc, wukv)


def _mla_attn_body(q_ref, k_ref, v_ref, o_ref, *, tq):
    S = q_ref.shape[0]
    nq = S // tq
    row = lax.broadcasted_iota(jnp.int32, (tq, tq), 0)
    col = lax.broadcasted_iota(jnp.int32, (tq, tq), 1)
    causal = col <= row
    lane = lax.broadcasted_iota(jnp.int32, (1, LANES), 1)
    first = lane < MLA_V
    for qi in range(nq):
        rows = slice(qi * tq, (qi + 1) * tq)
        outs = []
        for hh in range(2):
            hs = slice(hh * LANES, (hh + 1) * LANES)
            q = q_ref[rows, hs]
            m = jnp.full((tq, 1), -jnp.inf, F32)
            l = jnp.zeros((tq, 1), F32)
            acc = jnp.zeros((tq, LANES), F32)
            for ki in range(qi + 1):
                krows = slice(ki * tq, (ki + 1) * tq)
                s = lax.dot_general(q, k_ref[krows, hs], (((1,), (1,)), ((), ())),
                                    preferred_element_type=F32)
                if ki == qi:
                    s = jnp.where(causal, s, NEG)
                m_new = jnp.maximum(m, jnp.max(s, axis=-1, keepdims=True))
                p = jnp.exp2(s - m_new)
                alpha = jnp.exp2(m - m_new)
                l = alpha * l + jnp.sum(p, axis=-1, keepdims=True)
                acc = alpha * acc + jnp.dot(p.astype(BF16), v_ref[krows, :], preferred_element_type=F32)
                m = m_new
            outs.append(acc * (1.0 / l))
        o_ref[rows, :] = jnp.where(first, outs[0], outs[1]).astype(BF16)


def _mla_attn(qf, kf, v, B, S, tq):
    nq = qf.shape[1]
    pairs = MLA_HEADS // 2
    q3 = qf.reshape(B, S, nq)
    k3 = kf.reshape(B, S, nq)
    v3 = v.reshape(B, S, MLA_HEADS * MLA_V)
    out = pl.pallas_call(
        functools.partial(_mla_attn_body, tq=tq),
        grid=(B, pairs),
        in_specs=[pl.BlockSpec((None, S, 2 * LANES), lambda b, p: (b, 0, p)),
                  pl.BlockSpec((None, S, 2 * LANES), lambda b, p: (b, 0, p)),
                  pl.BlockSpec((None, S, LANES), lambda b, p: (b, 0, p))],
        out_specs=pl.BlockSpec((None, S, LANES), lambda b, p: (b, 0, p)),
        out_shape=jax.ShapeDtypeStruct((B, S, MLA_HEADS * MLA_V), BF16),
        compiler_params=_params(2),
        name="mla_attn",
    )(q3, k3, v3)
    return out.reshape(B * S, MLA_HEADS * MLA_V)


def _dil_attn_body(q_ref, k_ref, v_ref, gq_ref, gk_ref, bd_ref, o_ref, m_ref, l_ref,
                   qn_scr, kn_scr, va_scr, vb_scr, bias_scr, *, dil, L, nsub):
    step = pl.program_id(1)
    lane = lax.broadcasted_iota(jnp.int32, (1, LANES), 1)
    lo = lane < DIL_HEAD_DIM
    inv_dh = 1.0 / DIL_HEAD_DIM
    bd = bd_ref[...]
    pairs = DIL_HEADS // 2

    def rows(j, start, n):
        if dil == 1:
            return pl.ds(start, n)
        return pl.ds(start * dil + step * nsub + j, n, stride=dil)

    def head_rsqrt(x):
        ssq = jnp.dot((x * x).astype(BF16), bd, preferred_element_type=F32)
        return lax.rsqrt(ssq * inv_dh + EPS)

    for j in range(nsub):
        for pb in range(pairs):
            sl = slice(pb * LANES, (pb + 1) * LANES)
            x = k_ref[pb, rows(j, 0, L), :].astype(F32)
            kn_scr[j, :, sl] = (x * head_rsqrt(x) * gk_ref[...]).astype(BF16)
            x = q_ref[pb, rows(j, 0, L), :].astype(F32)
            xn = x * head_rsqrt(x) * gq_ref[...]
            qn_scr[j, :, (2 * pb) * LANES:(2 * pb + 1) * LANES] = jnp.where(lo, xn, 0.0).astype(BF16)
            qn_scr[j, :, (2 * pb + 1) * LANES:(2 * pb + 2) * LANES] = jnp.where(lo, 0.0, xn).astype(BF16)
            v = v_ref[pb, rows(j, 0, L), :].astype(F32)
            va_scr[j, :, sl] = jnp.where(lo, v, 1.0).astype(BF16)
            vb_scr[j, :, sl] = jnp.where(lo, 1.0, v).astype(BF16)

    @pl.when(jnp.logical_and(pl.program_id(0) == 0, step == 0))
    def _():
        qi = lax.broadcasted_iota(jnp.int32, (BAND, 2 * BAND), 0)
        kj = lax.broadcasted_iota(jnp.int32, (BAND, 2 * BAND), 1)
        delta = qi + BAND - kj
        valid = jnp.logical_and(delta >= 0, delta <= BAND)
        dist = (delta * dil).astype(F32)
        for h in range(DIL_HEADS):
            slope = 2.0 ** (-8.0 * (h + 1) / DIL_HEADS) * LOG2E
            bias_scr[h] = jnp.where(valid, -slope * dist, NEG)

    def run_blocks(blocks):
        tasks = [(bi, pb, hh) for bi in range(len(blocks)) for pb in range(pairs) for hh in range(2)]
        scores = {}
        mrow = [jnp.zeros((BAND, LANES), F32) for _ in blocks]
        lrow = [jnp.ones((BAND, LANES), F32) for _ in blocks]
        acc_first = {}
        for i in range(len(tasks) + DIL_LOOKAHEAD):
            if i < len(tasks):
                bi, pb, hh = tasks[i]
                j, qs, ks, nk = blocks[bi]
                h = 2 * pb + hh
                qh = qn_scr[j, pl.ds(qs, BAND), h * LANES:(h + 1) * LANES]
                kp = kn_scr[j, pl.ds(ks, nk), pb * LANES:(pb + 1) * LANES]
                s = lax.dot_general(qh, kp, (((1,), (1,)), ((), ())), preferred_element_type=F32)
                scores[i] = s + bias_scr[h, :, 2 * BAND - nk:]
            if i >= DIL_LOOKAHEAD:
                t = i - DIL_LOOKAHEAD
                bi, pb, hh = tasks[t]
                j, qs, ks, nk = blocks[bi]
                s = scores.pop(t)
                m = jnp.max(s, axis=-1, keepdims=True)
                p = jnp.exp2(s - m).astype(BF16)
                vaug = (va_scr if hh == 0 else vb_scr)[j, pl.ds(ks, nk), pb * LANES:(pb + 1) * LANES]
                acc = jnp.dot(p, vaug, preferred_element_type=F32)
                pos = DIL_HEAD_DIM + pb if hh == 0 else pb
                mrow[bi] = jnp.where(lane == pos, m, mrow[bi])
                lrow[bi] = jnp.where(lane == pos, acc, lrow[bi])
                if hh == 0:
                    acc_first[bi, pb] = acc
                else:
                    o_ref[pb, rows(j, qs, BAND), :] = jnp.where(
                        lo, acc_first.pop((bi, pb)), acc).astype(o_ref.dtype)
                    if pb == pairs - 1:
                        m_ref[rows(j, qs, BAND), :] = mrow[bi]
                        l_ref[rows(j, qs, BAND), :] = lrow[bi]

    nb = L // BAND
    first = [(j, 0, 0, BAND) for j in range(nsub)]
    rest = nb - 1
    if rest <= DIL_UNROLL:
        run_blocks(first + [(0, b * BAND, (b - 1) * BAND, 2 * BAND) for b in range(1, nb)])
    else:
        run_blocks(first)

        def body(i, carry):
            blocks = []
            for t in range(DIL_UNROLL):
                blk = 1 + DIL_UNROLL * i + t
                blocks.append((0, pl.multiple_of(blk * BAND, BAND), pl.multiple_of((blk - 1) * BAND, BAND),
                               2 * BAND))
            run_blocks(blocks)
            return carry
        lax.fori_loop(0, rest // DIL_UNROLL, body, 0)
        done = 1 + (rest // DIL_UNROLL) * DIL_UNROLL
        if done < nb:
            run_blocks([(0, b * BAND, (b - 1) * BAND, 2 * BAND) for b in range(done, nb)])


def _dil_attn(slabs, first_slab_block, gq2, gk2, bd, B, S, g, nsub):
    window, dil = DIL_PATTERNS[g]
    assert window // dil == BAND
    L = S // dil
    assert L % BAND == 0 and dil % nsub == 0 and (nsub == 1 or L == BAND)
    T = B * S
    pairs = DIL_HEADS // 2

    def in_spec(which):
        return pl.BlockSpec((pairs, S, LANES), lambda b, r: (first_slab_block + which, b, 0))

    stat = pl.BlockSpec((S, LANES), lambda b, r: (b, 0))
    return pl.pallas_call(
        functools.partial(_dil_attn_body, dil=dil, L=L, nsub=nsub),
        grid=(B, dil // nsub),
        in_specs=[in_spec(0), in_spec(1), in_spec(2), _const_spec((1, LANES)), _const_spec((1, LANES)),
                  _const_spec((LANES, LANES))],
        out_specs=[pl.BlockSpec((pairs, S, LANES), lambda b, r: (0, b, 0)), stat, stat],
        out_shape=[jax.ShapeDtypeStruct((pairs, T, LANES), slabs.dtype),
                   jax.ShapeDtypeStruct((T, LANES), F32),
                   jax.ShapeDtypeStruct((T, LANES), F32)],
        scratch_shapes=[pltpu.VMEM((nsub, L, DIL_HEADS * LANES), BF16),
                        pltpu.VMEM((nsub, L, DIL_GROUP_COLS), BF16),
                        pltpu.VMEM((nsub, L, DIL_GROUP_COLS), BF16),
                        pltpu.VMEM((nsub, L, DIL_GROUP_COLS), BF16),
                        pltpu.VMEM((DIL_HEADS, BAND, 2 * BAND), F32)],
        compiler_params=_params(2),
        name=f"dil_attn_g{g}",
    )(slabs, slabs, slabs, gq2, gk2, bd)


def _outproj_body(x_ref, g1_ref, wg_ref, bg_ref, om_ref, od0_ref, od1_ref, od2_ref,
                  m0_ref, m1_ref, m2_ref, l0_ref, l1_ref, l2_ref, ex_ref, wom_ref, wod_ref, wout_ref,
                  g2_ref, wrh_ref, wrl_ref, br_ref, x1_ref, oh_ref, cnt_ref):
    D = x_ref.shape[1]
    x = x_ref[...]
    h = _rms(x, g1_ref[...]).astype(BF16)
    gp = jnp.dot(h, wg_ref[...], preferred_element_type=F32) + bg_ref[...]
    gate_a = _sigmoid(gp[:, :D])
    gate_b = _sigmoid(gp[:, D:])

    m0, m1, m2 = m0_ref[...], m1_ref[...], m2_ref[...]
    mx = jnp.maximum(jnp.maximum(m0, m1), m2)
    e0, e1, e2 = jnp.exp2(m0 - mx), jnp.exp2(m1 - mx), jnp.exp2(m2 - mx)
    inv = 1.0 / (e0 * l0_ref[...] + e1 * l1_ref[...] + e2 * l2_ref[...])
    ex = ex_ref[...]

    def spread(w, od_ref):
        hi, lo = _split_bf16(w)
        wide = (jnp.dot(hi, ex, preferred_element_type=F32) + jnp.dot(lo, ex, preferred_element_type=F32))
        o = jnp.concatenate([od_ref[p] for p in range(od_ref.shape[0])], axis=-1)
        return wide * o.astype(F32)

    od = spread(e0 * inv, od0_ref) + spread(e1 * inv, od1_ref) + spread(e2 * inv, od2_ref)

    a = jnp.dot(om_ref[...], wom_ref[...], preferred_element_type=F32)
    b = jnp.dot(od.astype(BF16), wod_ref[...], preferred_element_type=F32)
    merged = gate_a * a + gate_b * b
    x1 = x + jnp.dot(merged.astype(BF16), wout_ref[...], preferred_element_type=F32)
    x1_ref[:, :D] = x1
    h2 = _rms(x1, g2_ref[...])

    lane = lax.broadcasted_iota(jnp.int32, (1, LANES), 1)
    lane_f = lane.astype(F32)
    big = float(LANES)
    h2h, h2l = _split_bf16(h2)
    wrh = wrh_ref[...]
    logits = (jnp.dot(h2h, wrh, preferred_element_type=F32)
              + jnp.dot(h2h, wrl_ref[...], preferred_element_type=F32)
              + jnp.dot(h2l, wrh, preferred_element_type=F32)) + br_ref[...]
    is_group = jnp.logical_and(lane >= N_EXPERTS, lane < N_EXPERTS + N_EXPERT_GROUPS)
    lg = jnp.where(is_group, logits, -jnp.inf)
    gmax = jnp.max(lg, axis=-1, keepdims=True)
    gi = jnp.min(jnp.where(lg == gmax, lane_f, big), axis=-1, keepdims=True) - float(N_EXPERTS)
    g_p = 1.0 / jnp.sum(jnp.exp(lg - gmax), axis=-1, keepdims=True)
    lo_e = gi * EXPERTS_PER_GROUP
    sel = jnp.logical_and(lane_f >= lo_e, lane_f < lo_e + EXPERTS_PER_GROUP)
    els = jnp.where(sel, logits, -jnp.inf)
    m1 = jnp.max(els, axis=-1, keepdims=True)
    i1 = jnp.min(jnp.where(els == m1, lane_f, big), axis=-1, keepdims=True)
    els2 = jnp.where(lane_f == i1, -jnp.inf, els)
    m2 = jnp.max(els2, axis=-1, keepdims=True)
    i2 = jnp.min(jnp.where(els2 == m2, lane_f, big), axis=-1, keepdims=True)
    t = jnp.exp(m2 - m1)
    w1 = g_p / (1.0 + t)
    w2 = w1 * t
    swap = i2 < i1
    a = jnp.where(swap, i2, i1) - lo_e
    b = jnp.where(swap, i1, i2) - lo_e
    pair = a * (2.0 * EXPERTS_PER_GROUP - 1.0 - a) * 0.5 + (b - a - 1.0)
    cls = gi * float(PAIRS_PER_GROUP) + pair
    x1_ref[:, D:] = jnp.where(lane == 0, jnp.where(swap, w2, w1), jnp.where(lane == 1, jnp.where(swap, w1, w2), 0.0))
    onehot = jnp.where(lane_f == cls, 1.0, 0.0)
    oh_ref[...] = onehot.astype(BF16)

    @pl.when(pl.program_id(0) == 0)
    def _():
        cnt_ref[...] = jnp.zeros_like(cnt_ref)

    cnt_ref[0:1, :] += jnp.sum(onehot, axis=0, keepdims=True)


def _outproj(x2, g1, wg, bg, om, ods, ms, ls, ex, wom, wod, wout, g2, wrh, wrl, br, tm):
    T, D = x2.shape
    row = lambda n: pl.BlockSpec((tm, n), lambda i: (i, 0))
    slab = pl.BlockSpec((DIL_HEADS // 2, tm, LANES), lambda i: (0, i, 0))
    return pl.pallas_call(
        _outproj_body,
        grid=(T // tm,),
        in_specs=[row(D), _const_spec((1, D)), _const_spec(wg.shape), _const_spec((1, 2 * D)),
                  row(om.shape[1]), slab, slab, slab,
                  row(LANES), row(LANES), row(LANES), row(LANES), row(LANES), row(LANES),
                  _const_spec(ex.shape),
                  _const_spec(wom.shape), _const_spec(wod.shape), _const_spec(wout.shape),
                  _const_spec((1, D)), _const_spec(wrh.shape), _const_spec(wrl.shape),
                  _const_spec((1, LANES))],
        out_specs=[row(D + LANES), row(LANES), _const_spec((8, LANES))],
        out_shape=[jax.ShapeDtypeStruct((T, D + LANES), F32),
                   jax.ShapeDtypeStruct((T, LANES), BF16),
                   jax.ShapeDtypeStruct((8, LANES), F32)],
        compiler_params=_params(1),
        name="outproj_router",
    )(x2, g1, wg, bg, om, ods[0], ods[1], ods[2], ms[0], ms[1], ms[2], ls[0], ls[1], ls[2], ex,
      wom, wod, wout, g2, wrh, wrl, br)


def _positions_body(oh_ref, offs_ref, ltri_ref, pos_ref, carry_scr):
    @pl.when(pl.program_id(0) == 0)
    def _():
        carry_scr[...] = jnp.zeros_like(carry_scr)

    oh = oh_ref[...]
    before = jnp.dot(ltri_ref[...], oh, preferred_element_type=F32) + carry_scr[0:1, :]
    val = jnp.where(oh > 0, before + offs_ref[...], 0.0)
    hi = jnp.floor(val * (1.0 / 256.0))
    lo = val - hi * 256.0
    ones = jnp.ones((8, LANES), BF16)
    nt = (((1,), (1,)), ((), ()))
    r = (256.0 * lax.dot_general(ones, hi.astype(BF16), nt, preferred_element_type=F32)
         + lax.dot_general(ones, lo.astype(BF16), nt, preferred_element_type=F32))
    pos_ref[...] = r.astype(jnp.int32)
    carry_scr[...] += jnp.dot(jnp.ones((8, oh.shape[0]), BF16), oh, preferred_element_type=F32)


def _positions(onehot, offs, chunk):
    T = onehot.shape[0]
    ltri = jnp.asarray(np.tril(np.ones((chunk, chunk), np.float32), -1), dtype=BF16)
    pos = pl.pallas_call(
        _positions_body,
        grid=(T // chunk,),
        in_specs=[pl.BlockSpec((chunk, LANES), lambda i: (i, 0)), _const_spec((1, LANES)),
                  _const_spec((chunk, chunk))],
        out_specs=pl.BlockSpec((None, 8, chunk), lambda i: (i, 0, 0)),
        out_shape=jax.ShapeDtypeStruct((T // chunk, 8, chunk), jnp.int32),
        scratch_shapes=[pltpu.VMEM((8, LANES), F32)],
        compiler_params=_params(1),
        name="moe_positions",
    )(onehot, offs, ltri)
    return pos[:, 0, :].reshape(T)


def _permute_rows_body(pos_ref, src_ref, *rest, tm, chunk, gather):
    dst_ref, sems = rest[-2:]

    def row_copy(r, slot):
        p = pos_ref[0, r]
        s_row, d_row = (p, r) if gather else (r, p)
        return pltpu.make_async_copy(src_ref.at[pl.ds(s_row, 1), :], dst_ref.at[pl.ds(d_row, 1), :],
                                     sems.at[slot])

    def start(r, carry, slot):
        row_copy(r, slot).start()
        return carry

    def wait(r, carry, slot):
        row_copy(r, slot).wait()
        return carry

    nchunk = tm // chunk
    for c in range(nchunk + 1):
        if c < nchunk:
            lax.fori_loop(c * chunk, (c + 1) * chunk, functools.partial(start, slot=c % 2), 0, unroll=8)
        if c >= 1:
            lax.fori_loop((c - 1) * chunk, c * chunk, functools.partial(wait, slot=(c - 1) % 2), 0, unroll=8)


def _permute_rows(pos, src, dst_rows, gather, tm=1024, chunk=256):
    T = pos.shape[0]
    pos3 = pos.reshape(T // tm, 1, tm)
    width = src.shape[1]
    args = [pos3, src]
    tile = pl.BlockSpec((tm, width), lambda i: (i, 0))
    in_specs = [pl.BlockSpec((None, 1, tm), lambda i: (i, 0, 0), memory_space=pltpu.SMEM)]
    aliases = {}
    if gather:
        in_specs.append(pl.BlockSpec(memory_space=pl.ANY))
        out_spec = tile
    else:
        args.append(jnp.zeros((dst_rows, width), src.dtype))
        in_specs += [tile, pl.BlockSpec(memory_space=pl.ANY)]
        out_spec = pl.BlockSpec(memory_space=pl.ANY)
        aliases = {2: 0}
    return pl.pallas_call(
        functools.partial(_permute_rows_body, tm=tm, chunk=chunk, gather=gather),
        grid=(T // tm,),
        in_specs=in_specs,
        out_specs=out_spec,
        out_shape=jax.ShapeDtypeStruct((dst_rows, width), src.dtype),
        scratch_shapes=[pltpu.SemaphoreType.DMA((2,))],
        input_output_aliases=aliases,
        compiler_params=_params(1),
        name="moe_gather_rows" if gather else "moe_dispatch_rows",
    )(*args)


def _pair_experts_body(ea_ref, eb_ref, blk_ref, nused_ref, xs_ref, g2_ref, wga_ref, wgb_ref, wua_ref, wub_ref,
                       wda_ref, wdb_ref, ys_ref):
    del ea_ref, eb_ref, blk_ref
    D = ys_ref.shape[1]

    @pl.when(pl.program_id(0) < nused_ref[0])
    def _():
        x1 = xs_ref[:, :D]
        w_a = xs_ref[:, D:D + 1]
        w_b = xs_ref[:, D + 1:D + 2]
        h = _rms(x1, g2_ref[...]).astype(BF16)

        def ffn(wg_ref, wu_ref, w):
            a = jnp.dot(h, wg_ref[...], preferred_element_type=F32)
            b = jnp.dot(h, wu_ref[...], preferred_element_type=F32)
            return (a * _sigmoid(a) * b * w).astype(BF16)

        y = (jnp.dot(ffn(wga_ref, wua_ref, w_a), wda_ref[...], preferred_element_type=F32)
             + jnp.dot(ffn(wgb_ref, wub_ref, w_b), wdb_ref[...], preferred_element_type=F32))
        ys_ref[...] = x1 + y

    @pl.when(pl.program_id(0) >= nused_ref[0])
    def _():
        ys_ref[...] = jnp.zeros_like(ys_ref)


def _pair_experts(tile_ea, tile_eb, tile_blk, n_used, xs, g2, wg, wu, wd, tm):
    rows, width = xs.shape
    E, D, F = wg.shape
    nt = rows // tm
    by_a = lambda i, ea, eb, blk, nu: (ea[i], 0, 0)
    by_b = lambda i, ea, eb, blk, nu: (eb[i], 0, 0)
    tile = lambda i, ea, eb, blk, nu: (blk[i], 0)
    grid_spec = pltpu.PrefetchScalarGridSpec(
        num_scalar_prefetch=4,
        grid=(nt,),
        in_specs=[pl.BlockSpec((tm, width), tile),
                  pl.BlockSpec((1, D), lambda i, ea, eb, blk, nu: (0, 0)),
                  pl.BlockSpec((None, D, F), by_a), pl.BlockSpec((None, D, F), by_b),
                  pl.BlockSpec((None, D, F), by_a), pl.BlockSpec((None, D, F), by_b),
                  pl.BlockSpec((None, F, D), by_a), pl.BlockSpec((None, F, D), by_b)],
        out_specs=pl.BlockSpec((tm, D), lambda i, ea, eb, blk, nu: (i, 0)),
    )
    return pl.pallas_call(
        _pair_experts_body,
        grid_spec=grid_spec,
        out_shape=jax.ShapeDtypeStruct((rows, D), F32),
        compiler_params=_params(1),
        name="moe_pair_experts",
    )(tile_ea, tile_eb, tile_blk, n_used, xs, g2, wg, wg, wu, wu, wd, wd)


def _rot_half_cols(w):
    half = w.shape[-1] // 2
    return jnp.concatenate([-w[..., half:], w[..., :half]], axis=-1)


def _swap_halves(g):
    half = g.shape[-1] // 2
    return jnp.concatenate([g[..., half:], g[..., :half]], axis=-1)


def kernel(x, positions, norm_attn, w_in, b_gate, norm_ckv, w_ukv, q_norm_mla, k_norm_mla, q_norm_dil, k_norm_dil, w_o_mla, w_o_dil, w_out, norm_ffn, w_router_group, b_router_group, w_router_expert, b_router_expert, w_gate, w_up, w_down):
    B, S, D = x.shape
    T = B * S
    depth = norm_attn.shape[0]
    q_cols = MLA_HEADS * MLA_QK
    c0 = q_cols
    c1 = c0 + MLA_KV_RANK
    c2 = c1 + MLA_ROPE
    c3 = c2 + 3 * N_DIL_GROUPS * DIL_GROUP_COLS

    half = MLA_ROPE // 2
    freqs = ROPE_THETA ** (-jnp.arange(half, dtype=F32) / half)
    ang = positions.astype(F32).reshape(T, 1) * freqs
    cos, sin = jnp.cos(ang), jnp.sin(ang)
    cs = jnp.concatenate([jnp.ones((T, MLA_NOPE), F32), cos, cos, sin, sin], axis=-1)

    ex_np = np.zeros((LANES, DIL_GROUP_COLS), np.float32)
    for p in range(DIL_HEADS // 2):
        ex_np[DIL_HEAD_DIM + p, (2 * p) * DIL_HEAD_DIM:(2 * p + 1) * DIL_HEAD_DIM] = 1.0
        ex_np[p, (2 * p + 1) * DIL_HEAD_DIM:(2 * p + 2) * DIL_HEAD_DIM] = 1.0
    ex = jnp.asarray(ex_np, dtype=BF16)
    bd = jnp.asarray(np.kron(np.eye(2, dtype=np.float32), np.ones((DIL_HEAD_DIM, DIL_HEAD_DIM), np.float32)),
                     dtype=BF16)

    xcur = x.reshape(T, D)
    for l in range(depth):
        w = w_in[l]
        wq = w[:, :c0].reshape(D, MLA_HEADS, MLA_QK)
        wq = jnp.concatenate([wq, _rot_half_cols(wq[:, :, MLA_NOPE:])], axis=-1).reshape(D, MLA_HEADS * LANES)
        wkr = w[:, c1:c2]
        wc = jnp.concatenate([w[:, c0:c1], jnp.zeros((D, MLA_NOPE), F32), wkr, _rot_half_cols(wkr)], axis=-1)
        wd = w[:, c2:c3].reshape(D, 3, N_DIL_GROUPS, DIL_GROUP_COLS).transpose(0, 2, 1, 3).reshape(D, c3 - c2)
        wgate = w[:, c3:]

        wukv = w_ukv[l].reshape(MLA_KV_RANK, MLA_HEADS, MLA_NOPE + MLA_V)
        wk = jnp.concatenate([wukv[:, :, :MLA_NOPE], jnp.zeros((MLA_KV_RANK, MLA_HEADS, LANES - MLA_NOPE), F32)], axis=-1)
        wukv2 = jnp.concatenate([wk.reshape(MLA_KV_RANK, MLA_HEADS * LANES),
                                 wukv[:, :, MLA_NOPE:].reshape(MLA_KV_RANK, MLA_HEADS * MLA_V)], axis=-1)

        qn, kn = q_norm_mla[l], k_norm_mla[l]
        scale = MLA_QK ** -0.5 * LOG2E
        gq = (jnp.concatenate([qn, _swap_halves(qn[MLA_NOPE:])]) * scale).reshape(1, LANES)
        gkn = jnp.concatenate([kn[:MLA_NOPE], jnp.zeros((LANES - MLA_NOPE,), F32)]).reshape(1, LANES)
        gkr = jnp.concatenate([jnp.zeros((MLA_NOPE,), F32), kn[MLA_NOPE:], _swap_halves(kn[MLA_NOPE:])]).reshape(1, LANES)

        q1, cc, d0, d12 = _inproj(xcur, norm_attn[l].reshape(1, D), wq.astype(BF16), wc.astype(BF16),
                                  wd.astype(BF16), tm=512)
        qf, kf, v = _mla_prep(q1, cc, cs, gq, gkn, gkr, norm_ckv[l].reshape(1, MLA_KV_RANK),
                              wukv2.astype(BF16), tm=512)
        o_mla = _mla_attn(qf, kf, v, B, S, tq=512)

        ods, ms, ls = [], [], []
        for g in range(N_DIL_GROUPS):
            gq2 = (jnp.tile(q_norm_dil[l, g], 2) * (DIL_HEAD_DIM ** -0.5 * LOG2E)).reshape(1, LANES)
            gk2 = jnp.tile(k_norm_dil[l, g], 2).reshape(1, LANES)
            if g == 0:
                o, m, den = _dil_attn(d0, 0, gq2, gk2, bd, B, S, g, nsub=1)
            else:
                o, m, den = _dil_attn(d12, (g - 1) * 3, gq2, gk2, bd, B, S, g, nsub=1 if g == 1 else 4)
            ods.append(o)
            ms.append(m)
            ls.append(den)

        pad_cols = lambda a: jnp.pad(a, ((0, 0), (0, LANES - a.shape[1])))
        wr = pad_cols(jnp.concatenate([w_router_expert[l], w_router_group[l]], axis=1))
        br = pad_cols(jnp.concatenate([b_router_expert[l], b_router_group[l]]).reshape(1, -1))
        wrh = wr.astype(BF16)
        wrl = (wr - wrh.astype(F32)).astype(BF16)
        x1w, onehot, cnt = _outproj(
            xcur, norm_attn[l].reshape(1, D), wgate.astype(BF16), b_gate[l].reshape(1, 2 * D),
            o_mla, ods, ms, ls, ex, w_o_mla[l].astype(BF16), w_o_dil[l].astype(BF16), w_out[l].astype(BF16),
            norm_ffn[l].reshape(1, D), wrh, wrl, br, tm=512)

        counts = cnt[0].astype(jnp.int32)
        tiles_per = (counts + MOE_TILE - 1) // MOE_TILE
        cum = jnp.cumsum(tiles_per)
        offs = ((cum - tiles_per) * MOE_TILE).astype(F32).reshape(1, LANES)
        n_tiles = T // MOE_TILE + N_CLASSES
        blk = jnp.minimum(jnp.arange(n_tiles, dtype=jnp.int32), cum[-1] - 1)
        tcls = jnp.minimum(jnp.sum((cum[None, :] <= blk[:, None]).astype(jnp.int32), axis=1), N_CLASSES - 1)
        tile_ea = jnp.asarray(_CLASS_EXPERT_A)[tcls]
        tile_eb = jnp.asarray(_CLASS_EXPERT_B)[tcls]

        pos = _positions(onehot, offs, chunk=512)
        xs = _permute_rows(pos, x1w, n_tiles * MOE_TILE, gather=False)
        ys = _pair_experts(tile_ea, tile_eb, blk, cum[-1:].astype(jnp.int32), xs, norm_ffn[l].reshape(1, D),
                           w_gate[l].astype(BF16), w_up[l].astype(BF16), w_down[l].astype(BF16), MOE_TILE)
        xcur = _permute_rows(pos, ys, T, gather=True)
    return xcur.reshape(B, S, D)
```

```python
import functools
import math

import jax
import jax.numpy as jnp
import numpy as np
from jax import lax
from jax.experimental import pallas as pl
from jax.experimental.pallas import tpu as pltpu

F32 = jnp.float32
BF16 = jnp.bfloat16
EPS = 1e-6
NEG = -1e30
LOG2E = math.log2(math.e)

LANES = 128
VMEM_LIMIT = 56 * 1024 * 1024

MLA_HEADS = 8
MLA_NOPE = 64
MLA_ROPE = 32
MLA_QK = MLA_NOPE + MLA_ROPE
MLA_V = 64
MLA_KV_RANK = 256
ROPE_THETA = 10000.0
DIL_PATTERNS = ((128, 1), (512, 4), (2048, 16))
N_DIL_GROUPS = 3
DIL_HEADS = 8
DIL_HEAD_DIM = 64
DIL_GROUP_COLS = DIL_HEADS * DIL_HEAD_DIM
BAND = 128
MLA_LOOKAHEAD = 2
DIL_LOOKAHEAD = 6
DIL_UNROLL = 3
N_EXPERT_GROUPS = 4
EXPERTS_PER_GROUP = 8
N_EXPERTS = N_EXPERT_GROUPS * EXPERTS_PER_GROUP
D_FF_EXPERT = 256
PAIRS_PER_GROUP = EXPERTS_PER_GROUP * (EXPERTS_PER_GROUP - 1) // 2
N_CLASSES = N_EXPERT_GROUPS * PAIRS_PER_GROUP
MOE_TILE = 256


def _class_expert_tables():
    ea, eb = [], []
    for g in range(N_EXPERT_GROUPS):
        for a in range(EXPERTS_PER_GROUP):
            for b in range(a + 1, EXPERTS_PER_GROUP):
                ea.append(g * EXPERTS_PER_GROUP + a)
                eb.append(g * EXPERTS_PER_GROUP + b)
    return np.asarray(ea, np.int32), np.asarray(eb, np.int32)


_CLASS_EXPERT_A, _CLASS_EXPERT_B = _class_expert_tables()


def _params(n_axes):
    return pltpu.CompilerParams(dimension_semantics=("arbitrary",) * n_axes,
                                vmem_limit_bytes=VMEM_LIMIT)


def _const_spec(shape):
    nd = len(shape)
    return pl.BlockSpec(shape, lambda *_: (0,) * nd)


def _rms(x, gain):
    return x * lax.rsqrt(jnp.mean(x * x, axis=-1, keepdims=True) + EPS) * gain


def _sigmoid(x):
    return 1.0 / (1.0 + jnp.exp(-x))


def _split_bf16(v):
    hi = v.astype(BF16)
    lo = (v - hi.astype(F32)).astype(BF16)
    return hi, lo


def _inproj_body(x_ref, g_ref, wq_ref, wc_ref, wd_ref, q_ref, c_ref, d0_ref, d12_ref):
    h = _rms(x_ref[...], g_ref[...]).astype(BF16)
    q_ref[...] = jnp.dot(h, wq_ref[...], preferred_element_type=F32).astype(BF16)
    c_ref[...] = jnp.dot(h, wc_ref[...], preferred_element_type=F32)
    d = jnp.dot(h, wd_ref[...], preferred_element_type=F32)
    n0 = d0_ref.shape[0]
    for j in range(n0):
        d0_ref[j] = d[:, j * LANES:(j + 1) * LANES].astype(BF16)
    for j in range(d12_ref.shape[0]):
        d12_ref[j] = d[:, (n0 + j) * LANES:(n0 + j + 1) * LANES]


def _inproj(x2, g, wq, wc, wd, tm):
    T, D = x2.shape
    nq, nc, nd = wq.shape[1], wc.shape[1], wd.shape[1]
    n0 = 3 * DIL_GROUP_COLS // LANES
    n12 = nd // LANES - n0
    return pl.pallas_call(
        _inproj_body,
        grid=(T // tm,),
        in_specs=[pl.BlockSpec((tm, D), lambda i: (i, 0)),
                  _const_spec((1, D)), _const_spec((D, nq)), _const_spec((D, nc)), _const_spec((D, nd))],
        out_specs=[pl.BlockSpec((tm, nq), lambda i: (i, 0)),
                   pl.BlockSpec((tm, nc), lambda i: (i, 0)),
                   pl.BlockSpec((n0, tm, LANES), lambda i: (0, i, 0)),
                   pl.BlockSpec((n12, tm, LANES), lambda i: (0, i, 0))],
        out_shape=[jax.ShapeDtypeStruct((T, nq), BF16),
                   jax.ShapeDtypeStruct((T, nc), F32),
                   jax.ShapeDtypeStruct((n0, T, LANES), BF16),
                   jax.ShapeDtypeStruct((n12, T, LANES), F32)],
        compiler_params=_params(1),
        name="inproj",
    )(x2, g, wq, wc, wd)


def _mla_prep_body(q_ref, c_ref, cs_ref, gq_ref, gkn_ref, gkr_ref, gc_ref, wukv_ref,
                   qf_ref, kf_ref, v_ref):
    lane = lax.broadcasted_iota(jnp.int32, (1, LANES), 1)
    in_qk = lane < MLA_QK
    mid = jnp.logical_and(lane >= MLA_NOPE, lane < MLA_QK)
    hi = lane >= MLA_QK
    cs = cs_ref[...]
    qmul = cs * gq_ref[...]
    for h in range(MLA_HEADS):
        sl = slice(h * LANES, (h + 1) * LANES)
        qh = q_ref[:, sl].astype(F32)
        ssq = jnp.sum(jnp.where(in_qk, qh * qh, 0.0), axis=-1, keepdims=True)
        r = lax.rsqrt(ssq * (1.0 / MLA_QK) + EPS)
        qf_ref[:, sl] = (qh * qmul * r).astype(BF16)

    ckv = c_ref[:, :MLA_KV_RANK]
    kr = c_ref[:, MLA_KV_RANK:]
    cn = _rms(ckv, gc_ref[...]).astype(BF16)
    kv = jnp.dot(cn, wukv_ref[...], preferred_element_type=F32)
    xr = kr * (cs * gkr_ref[...])
    rk2 = xr + jnp.where(mid, pltpu.roll(xr, 96, 1), jnp.where(hi, pltpu.roll(xr, 32, 1), 0.0))
    ssq_r = jnp.sum(jnp.where(mid, kr * kr, 0.0), axis=-1, keepdims=True)
    gkn = gkn_ref[...]
    for h in range(MLA_HEADS):
        sl = slice(h * LANES, (h + 1) * LANES)
        kn = kv[:, sl]
        ssq = jnp.sum(kn * kn, axis=-1, keepdims=True) + ssq_r
        r = lax.rsqrt(ssq * (1.0 / MLA_QK) + EPS)
        kf_ref[:, sl] = ((kn * gkn + rk2) * r).astype(BF16)
    v_ref[...] = kv[:, MLA_HEADS * LANES:].astype(BF16)


def _mla_prep(q1, c1, cs, gq, gkn, gkr, gc, wukv, tm):
    T = q1.shape[0]
    nq, nc, nkv = q1.shape[1], c1.shape[1], wukv.shape[1]
    nv = MLA_HEADS * MLA_V
    return pl.pallas_call(
        _mla_prep_body,
        grid=(T // tm,),
        in_specs=[pl.BlockSpec((tm, nq), lambda i: (i, 0)),
                  pl.BlockSpec((tm, nc), lambda i: (i, 0)),
                  pl.BlockSpec((tm, LANES), lambda i: (i, 0)),
                  _const_spec((1, LANES)), _const_spec((1, LANES)), _const_spec((1, LANES)),
                  _const_spec((1, MLA_KV_RANK)), _const_spec((MLA_KV_RANK, nkv))],
        out_specs=[pl.BlockSpec((tm, nq), lambda i: (i, 0)),
                   pl.BlockSpec((tm, nq), lambda i: (i, 0)),
                   pl.BlockSpec((tm, nv), lambda i: (i, 0))],
        out_shape=[jax.ShapeDtypeStruct((T, nq), BF16),
                   jax.ShapeDtypeStruct((T, nq), BF16),
                   jax.ShapeDtypeStruct((T, nv), BF16)],
        compiler_params=_params(1),
        name="mla_prep",
    )(q1, c1, cs, gq, gkn, gkr, gc, wukv)


def _mla_attn_body(q_ref, k_ref, v_ref, o_ref, *, tq):
    S = q_ref.shape[0]
    nq = S // tq
    row = lax.broadcasted_iota(jnp.int32, (tq, tq), 0)
    col = lax.broadcasted_iota(jnp.int32, (tq, tq), 1)
    causal = col <= row
    lane = lax.broadcasted_iota(jnp.int32, (1, LANES), 1)
    first = lane < MLA_V
    tasks = [(qi, ki, hh) for qi in range(nq) for ki in range(qi + 1) for hh in range(2)]
    scores, state, done = {}, {}, {}
    for i in range(len(tasks) + MLA_LOOKAHEAD):
        if i < len(tasks):
            qi, ki, hh = tasks[i]
            hs = slice(hh * LANES, (hh + 1) * LANES)
            s = lax.dot_general(q_ref[qi * tq:(qi + 1) * tq, hs], k_ref[ki * tq:(ki + 1) * tq, hs],
                                (((1,), (1,)), ((), ())), preferred_element_type=F32)
            scores[i] = jnp.where(causal, s, NEG) if ki == qi else s
        if i >= MLA_LOOKAHEAD:
            t = i - MLA_LOOKAHEAD
            qi, ki, hh = tasks[t]
            s = scores.pop(t)
            if ki == 0:
                m = jnp.max(s, axis=-1, keepdims=True)
                p = jnp.exp2(s - m)
                l = jnp.sum(p, axis=-1, keepdims=True)
                acc = jnp.dot(p.astype(BF16), v_ref[0:tq, :], preferred_element_type=F32)
            else:
                m_old, l_old, acc_old = state[qi, hh]
                m = jnp.maximum(m_old, jnp.max(s, axis=-1, keepdims=True))
                p = jnp.exp2(s - m)
                alpha = jnp.exp2(m_old - m)
                l = alpha * l_old + jnp.sum(p, axis=-1, keepdims=True)
                acc = alpha * acc_old + jnp.dot(p.astype(BF16), v_ref[ki * tq:(ki + 1) * tq, :],
                                                preferred_element_type=F32)
            state[qi, hh] = (m, l, acc)
            if ki == qi:
                done[qi, hh] = acc * (1.0 / l)
                del state[qi, hh]
                if hh == 1:
                    o_ref[qi * tq:(qi + 1) * tq, :] = jnp.where(
                        first, done.pop((qi, 0)), done.pop((qi, 1))).astype(BF16)


def _mla_attn(qf, kf, v, B, S, tq):
    nq = qf.shape[1]
    pairs = MLA_HEADS // 2
    q3 = qf.reshape(B, S, nq)
    k3 = kf.reshape(B, S, nq)
    v3 = v.reshape(B, S, MLA_HEADS * MLA_V)
    out = pl.pallas_call(
        functools.partial(_mla_attn_body, tq=tq),
        grid=(B, pairs),
        in_specs=[pl.BlockSpec((None, S, 2 * LANES), lambda b, p: (b, 0, p)),
                  pl.BlockSpec((None, S, 2 * LANES), lambda b, p: (b, 0, p)),
                  pl.BlockSpec((None, S, LANES), lambda b, p: (b, 0, p))],
        out_specs=pl.BlockSpec((None, S, LANES), lambda b, p: (b, 0, p)),
        out_shape=jax.ShapeDtypeStruct((B, S, MLA_HEADS * MLA_V), BF16),
        compiler_params=_params(2),
        name="mla_attn",
    )(q3, k3, v3)
    return out.reshape(B * S, MLA_HEADS * MLA_V)


def _dil_attn_body(q_ref, k_ref, v_ref, gq_ref, gk_ref, bd_ref, o_ref, m_ref, l_ref,
                   qn_scr, kn_scr, va_scr, vb_scr, bias_scr, *, dil, L, nsub):
    step = pl.program_id(1)
    lane = lax.broadcasted_iota(jnp.int32, (1, LANES), 1)
    lo = lane < DIL_HEAD_DIM
    inv_dh = 1.0 / DIL_HEAD_DIM
    bd = bd_ref[...]
    pairs = DIL_HEADS // 2

    def rows(j, start, n):
        if dil == 1:
            return pl.ds(start, n)
        return pl.ds(start * dil + step * nsub + j, n, stride=dil)

    def head_rsqrt(x):
        ssq = jnp.dot((x * x).astype(BF16), bd, preferred_element_type=F32)
        return lax.rsqrt(ssq * inv_dh + EPS)

    for j in range(nsub):
        for pb in range(pairs):
            sl = slice(pb * LANES, (pb + 1) * LANES)
            x = k_ref[pb, rows(j, 0, L), :].astype(F32)
            kn_scr[j, :, sl] = (x * head_rsqrt(x) * gk_ref[...]).astype(BF16)
            x = q_ref[pb, rows(j, 0, L), :].astype(F32)
            xn = x * head_rsqrt(x) * gq_ref[...]
            qn_scr[j, :, (2 * pb) * LANES:(2 * pb + 1) * LANES] = jnp.where(lo, xn, 0.0).astype(BF16)
            qn_scr[j, :, (2 * pb + 1) * LANES:(2 * pb + 2) * LANES] = jnp.where(lo, 0.0, xn).astype(BF16)
            v = v_ref[pb, rows(j, 0, L), :].astype(F32)
            va_scr[j, :, sl] = jnp.where(lo, v, 1.0).astype(BF16)
            vb_scr[j, :, sl] = jnp.where(lo, 1.0, v).astype(BF16)

    @pl.when(jnp.logical_and(pl.program_id(0) == 0, step == 0))
    def _():
        qi = lax.broadcasted_iota(jnp.int32, (BAND, 2 * BAND), 0)
        kj = lax.broadcasted_iota(jnp.int32, (BAND, 2 * BAND), 1)
        delta = qi + BAND - kj
        valid = jnp.logical_and(delta >= 0, delta <= BAND)
        dist = (delta * dil).astype(F32)
        for h in range(DIL_HEADS):
            slope = 2.0 ** (-8.0 * (h + 1) / DIL_HEADS) * LOG2E
            bias_scr[h] = jnp.where(valid, -slope * dist, NEG)

    def run_blocks(blocks):
        tasks = [(bi, pb, hh) for bi in range(len(blocks)) for pb in range(pairs) for hh in range(2)]
        scores = {}
        mrow = [jnp.zeros((BAND, LANES), F32) for _ in blocks]
        lrow = [jnp.ones((BAND, LANES), F32) for _ in blocks]
        acc_first = {}
        for i in range(len(tasks) + DIL_LOOKAHEAD):
            if i < len(tasks):
                bi, pb, hh = tasks[i]
                j, qs, ks, nk = blocks[bi]
                h = 2 * pb + hh
                qh = qn_scr[j, pl.ds(qs, BAND), h * LANES:(h + 1) * LANES]
                kp = kn_scr[j, pl.ds(ks, nk), pb * LANES:(pb + 1) * LANES]
                s = lax.dot_general(qh, kp, (((1,), (1,)), ((), ())), preferred_element_type=F32)
                scores[i] = s + bias_scr[h, :, 2 * BAND - nk:]
            if i >= DIL_LOOKAHEAD:
                t = i - DIL_LOOKAHEAD
                bi, pb, hh = tasks[t]
                j, qs, ks, nk = blocks[bi]
                s = scores.pop(t)
                m = jnp.max(s, axis=-1, keepdims=True)
                p = jnp.exp2(s - m).astype(BF16)
                vaug = (va_scr if hh == 0 else vb_scr)[j, pl.ds(ks, nk), pb * LANES:(pb + 1) * LANES]
                acc = jnp.dot(p, vaug, preferred_element_type=F32)
                pos = DIL_HEAD_DIM + pb if hh == 0 else pb
                mrow[bi] = jnp.where(lane == pos, m, mrow[bi])
                lrow[bi] = jnp.where(lane == pos, acc, lrow[bi])
                if hh == 0:
                    acc_first[bi, pb] = acc
                else:
                    o_ref[pb, rows(j, qs, BAND), :] = jnp.where(
                        lo, acc_first.pop((bi, pb)), acc).astype(o_ref.dtype)
                    if pb == pairs - 1:
                        m_ref[rows(j, qs, BAND), :] = mrow[bi]
                        l_ref[rows(j, qs, BAND), :] = lrow[bi]

    nb = L // BAND
    first = [(j, 0, 0, BAND) for j in range(nsub)]
    rest = nb - 1
    if rest <= DIL_UNROLL:
        run_blocks(first + [(0, b * BAND, (b - 1) * BAND, 2 * BAND) for b in range(1, nb)])
    else:
        run_blocks(first)

        def body(i, carry):
            blocks = []
            for t in range(DIL_UNROLL):
                blk = 1 + DIL_UNROLL * i + t
                blocks.append((0, pl.multiple_of(blk * BAND, BAND), pl.multiple_of((blk - 1) * BAND, BAND),
                               2 * BAND))
            run_blocks(blocks)
            return carry
        lax.fori_loop(0, rest // DIL_UNROLL, body, 0)
        done = 1 + (rest // DIL_UNROLL) * DIL_UNROLL
        if done < nb:
            run_blocks([(0, b * BAND, (b - 1) * BAND, 2 * BAND) for b in range(done, nb)])


def _dil_attn(slabs, first_slab_block, gq2, gk2, bd, B, S, g, nsub):
    window, dil = DIL_PATTERNS[g]
    assert window // dil == BAND
    L = S // dil
    assert L % BAND == 0 and dil % nsub == 0 and (nsub == 1 or L == BAND)
    T = B * S
    pairs = DIL_HEADS // 2

    def in_spec(which):
        return pl.BlockSpec((pairs, S, LANES), lambda b, r: (first_slab_block + which, b, 0))

    stat = pl.BlockSpec((S, LANES), lambda b, r: (b, 0))
    return pl.pallas_call(
        functools.partial(_dil_attn_body, dil=dil, L=L, nsub=nsub),
        grid=(B, dil // nsub),
        in_specs=[in_spec(0), in_spec(1), in_spec(2), _const_spec((1, LANES)), _const_spec((1, LANES)),
                  _const_spec((LANES, LANES))],
        out_specs=[pl.BlockSpec((pairs, S, LANES), lambda b, r: (0, b, 0)), stat, stat],
        out_shape=[jax.ShapeDtypeStruct((pairs, T, LANES), slabs.dtype),
                   jax.ShapeDtypeStruct((T, LANES), F32),
                   jax.ShapeDtypeStruct((T, LANES), F32)],
        scratch_shapes=[pltpu.VMEM((nsub, L, DIL_HEADS * LANES), BF16),
                        pltpu.VMEM((nsub, L, DIL_GROUP_COLS), BF16),
                        pltpu.VMEM((nsub, L, DIL_GROUP_COLS), BF16),
                        pltpu.VMEM((nsub, L, DIL_GROUP_COLS), BF16),
                        pltpu.VMEM((DIL_HEADS, BAND, 2 * BAND), F32)],
        compiler_params=_params(2),
        name=f"dil_attn_g{g}",
    )(slabs, slabs, slabs, gq2, gk2, bd)


def _outproj_body(x_ref, g1_ref, wg_ref, bg_ref, om_ref, od0_ref, od1_ref, od2_ref,
                  m0_ref, m1_ref, m2_ref, l0_ref, l1_ref, l2_ref, ex_ref, wom_ref, wod_ref, wout_ref,
                  g2_ref, wrh_ref, wrl_ref, br_ref, x1_ref, oh_ref, cnt_ref):
    D = x_ref.shape[1]
    x = x_ref[...]
    h = _rms(x, g1_ref[...]).astype(BF16)
    gp = jnp.dot(h, wg_ref[...], preferred_element_type=F32) + bg_ref[...]
    gate_a = _sigmoid(gp[:, :D])
    gate_b = _sigmoid(gp[:, D:])

    m0, m1, m2 = m0_ref[...], m1_ref[...], m2_ref[...]
    mx = jnp.maximum(jnp.maximum(m0, m1), m2)
    e0, e1, e2 = jnp.exp2(m0 - mx), jnp.exp2(m1 - mx), jnp.exp2(m2 - mx)
    inv = 1.0 / (e0 * l0_ref[...] + e1 * l1_ref[...] + e2 * l2_ref[...])
    ex = ex_ref[...]

    def spread(w, od_ref):
        hi, lo = _split_bf16(w)
        wide = (jnp.dot(hi, ex, preferred_element_type=F32) + jnp.dot(lo, ex, preferred_element_type=F32))
        o = jnp.concatenate([od_ref[p] for p in range(od_ref.shape[0])], axis=-1)
        return wide * o.astype(F32)

    od = spread(e0 * inv, od0_ref) + spread(e1 * inv, od1_ref) + spread(e2 * inv, od2_ref)

    a = jnp.dot(om_ref[...], wom_ref[...], preferred_element_type=F32)
    b = jnp.dot(od.astype(BF16), wod_ref[...], preferred_element_type=F32)
    merged = gate_a * a + gate_b * b
    x1 = x + jnp.dot(merged.astype(BF16), wout_ref[...], preferred_element_type=F32)
    x1_ref[:, :D] = x1
    h2 = _rms(x1, g2_ref[...])

    lane = lax.broadcasted_iota(jnp.int32, (1, LANES), 1)
    lane_f = lane.astype(F32)
    big = float(LANES)
    h2h, h2l = _split_bf16(h2)
    wrh = wrh_ref[...]
    logits = (jnp.dot(h2h, wrh, preferred_element_type=F32)
              + jnp.dot(h2h, wrl_ref[...], preferred_element_type=F32)
              + jnp.dot(h2l, wrh, preferred_element_type=F32)) + br_ref[...]
    is_group = jnp.logical_and(lane >= N_EXPERTS, lane < N_EXPERTS + N_EXPERT_GROUPS)
    lg = jnp.where(is_group, logits, -jnp.inf)
    gmax = jnp.max(lg, axis=-1, keepdims=True)
    gi = jnp.min(jnp.where(lg == gmax, lane_f, big), axis=-1, keepdims=True) - float(N_EXPERTS)
    g_p = 1.0 / jnp.sum(jnp.exp(lg - gmax), axis=-1, keepdims=True)
    lo_e = gi * EXPERTS_PER_GROUP
    sel = jnp.logical_and(lane_f >= lo_e, lane_f < lo_e + EXPERTS_PER_GROUP)
    els = jnp.where(sel, logits, -jnp.inf)
    m1 = jnp.max(els, axis=-1, keepdims=True)
    i1 = jnp.min(jnp.where(els == m1, lane_f, big), axis=-1, keepdims=True)
    els2 = jnp.where(lane_f == i1, -jnp.inf, els)
    m2 = jnp.max(els2, axis=-1, keepdims=True)
    i2 = jnp.min(jnp.where(els2 == m2, lane_f, big), axis=-1, keepdims=True)
    t = jnp.exp(m2 - m1)
    w1 = g_p / (1.0 + t)
    w2 = w1 * t
    swap = i2 < i1
    a = jnp.where(swap, i2, i1) - lo_e
    b = jnp.where(swap, i1, i2) - lo_e
    pair = a * (2.0 * EXPERTS_PER_GROUP - 1.0 - a) * 0.5 + (b - a - 1.0)
    cls = gi * float(PAIRS_PER_GROUP) + pair
    x1_ref[:, D:] = jnp.where(lane == 0, jnp.where(swap, w2, w1), jnp.where(lane == 1, jnp.where(swap, w1, w2), 0.0))
    onehot = jnp.where(lane_f == cls, 1.0, 0.0)
    oh_ref[...] = onehot.astype(BF16)

    @pl.when(pl.program_id(0) == 0)
    def _():
        cnt_ref[...] = jnp.zeros_like(cnt_ref)

    cnt_ref[0:1, :] += jnp.sum(onehot, axis=0, keepdims=True)


def _outproj(x2, g1, wg, bg, om, ods, ms, ls, ex, wom, wod, wout, g2, wrh, wrl, br, tm):
    T, D = x2.shape
    row = lambda n: pl.BlockSpec((tm, n), lambda i: (i, 0))
    slab = pl.BlockSpec((DIL_HEADS // 2, tm, LANES), lambda i: (0, i, 0))
    return pl.pallas_call(
        _outproj_body,
        grid=(T // tm,),
        in_specs=[row(D), _const_spec((1, D)), _const_spec(wg.shape), _const_spec((1, 2 * D)),
                  row(om.shape[1]), slab, slab, slab,
                  row(LANES), row(LANES), row(LANES), row(LANES), row(LANES), row(LANES),
                  _const_spec(ex.shape),
                  _const_spec(wom.shape), _const_spec(wod.shape), _const_spec(wout.shape),
                  _const_spec((1, D)), _const_spec(wrh.shape), _const_spec(wrl.shape),
                  _const_spec((1, LANES))],
        out_specs=[row(D + LANES), row(LANES), _const_spec((8, LANES))],
        out_shape=[jax.ShapeDtypeStruct((T, D + LANES), F32),
                   jax.ShapeDtypeStruct((T, LANES), BF16),
                   jax.ShapeDtypeStruct((8, LANES), F32)],
        compiler_params=_params(1),
        name="outproj_router",
    )(x2, g1, wg, bg, om, ods[0], ods[1], ods[2], ms[0], ms[1], ms[2], ls[0], ls[1], ls[2], ex,
      wom, wod, wout, g2, wrh, wrl, br)


def _positions_body(oh_ref, offs_ref, ltri_ref, pos_ref, carry_scr):
    @pl.when(pl.program_id(0) == 0)
    def _():
        carry_scr[...] = jnp.zeros_like(carry_scr)

    oh = oh_ref[...]
    before = jnp.dot(ltri_ref[...], oh, preferred_element_type=F32) + carry_scr[0:1, :]
    val = jnp.where(oh > 0, before + offs_ref[...], 0.0)
    hi = jnp.floor(val * (1.0 / 256.0))
    lo = val - hi * 256.0
    ones = jnp.ones((8, LANES), BF16)
    nt = (((1,), (1,)), ((), ()))
    r = (256.0 * lax.dot_general(ones, hi.astype(BF16), nt, preferred_element_type=F32)
         + lax.dot_general(ones, lo.astype(BF16), nt, preferred_element_type=F32))
    pos_ref[...] = r.astype(jnp.int32)
    carry_scr[...] += jnp.dot(jnp.ones((8, oh.shape[0]), BF16), oh, preferred_element_type=F32)


def _positions(onehot, offs, chunk):
    T = onehot.shape[0]
    ltri = jnp.asarray(np.tril(np.ones((chunk, chunk), np.float32), -1), dtype=BF16)
    pos = pl.pallas_call(
        _positions_body,
        grid=(T // chunk,),
        in_specs=[pl.BlockSpec((chunk, LANES), lambda i: (i, 0)), _const_spec((1, LANES)),
                  _const_spec((chunk, chunk))],
        out_specs=pl.BlockSpec((None, 8, chunk), lambda i: (i, 0, 0)),
        out_shape=jax.ShapeDtypeStruct((T // chunk, 8, chunk), jnp.int32),
        scratch_shapes=[pltpu.VMEM((8, LANES), F32)],
        compiler_params=_params(1),
        name="moe_positions",
    )(onehot, offs, ltri)
    return pos[:, 0, :].reshape(T)


def _permute_rows_body(pos_ref, src_ref, *rest, tm, chunk, gather):
    dst_ref, sems = rest[-2:]

    def row_copy(r, slot):
        p = pos_ref[0, r]
        s_row, d_row = (p, r) if gather else (r, p)
        return pltpu.make_async_copy(src_ref.at[pl.ds(s_row, 1), :], dst_ref.at[pl.ds(d_row, 1), :],
                                     sems.at[slot])

    def start_pair(k, carry, first, slot):
        row_copy(first + 2 * k, slot).start(priority=0)
        row_copy(first + 2 * k + 1, slot).start(priority=1)
        return carry

    def wait(r, carry, slot):
        row_copy(r, slot).wait()
        return carry

    nchunk = tm // chunk
    for c in range(nchunk + 1):
        if c < nchunk:
            lax.fori_loop(0, chunk // 2, functools.partial(start_pair, first=c * chunk, slot=c % 2), 0,
                          unroll=4)
        if c >= 1:
            lax.fori_loop((c - 1) * chunk, c * chunk, functools.partial(wait, slot=(c - 1) % 2), 0, unroll=8)


def _permute_rows(pos, src, dst_rows, gather, tm=1024, chunk=256):
    T = pos.shape[0]
    pos3 = pos.reshape(T // tm, 1, tm)
    width = src.shape[1]
    args = [pos3, src]
    tile = pl.BlockSpec((tm, width), lambda i: (i, 0))
    in_specs = [pl.BlockSpec((None, 1, tm), lambda i: (i, 0, 0), memory_space=pltpu.SMEM)]
    aliases = {}
    if gather:
        in_specs.append(pl.BlockSpec(memory_space=pl.ANY))
        out_spec = tile
    else:
        args.append(jnp.zeros((dst_rows, width), src.dtype))
        in_specs += [tile, pl.BlockSpec(memory_space=pl.ANY)]
        out_spec = pl.BlockSpec(memory_space=pl.ANY)
        aliases = {2: 0}
    return pl.pallas_call(
        functools.partial(_permute_rows_body, tm=tm, chunk=chunk, gather=gather),
        grid=(T // tm,),
        in_specs=in_specs,
        out_specs=out_spec,
        out_shape=jax.ShapeDtypeStruct((dst_rows, width), src.dtype),
        scratch_shapes=[pltpu.SemaphoreType.DMA((2,))],
        input_output_aliases=aliases,
        compiler_params=_params(1),
        name="moe_gather_rows" if gather else "moe_dispatch_rows",
    )(*args)


def _pair_experts_body(ea_ref, eb_ref, blk_ref, nused_ref, xs_ref, g2_ref, wga_ref, wgb_ref, wua_ref, wub_ref,
                       wda_ref, wdb_ref, ys_ref):
    del ea_ref, eb_ref, blk_ref
    D = ys_ref.shape[1]

    @pl.when(pl.program_id(0) < nused_ref[0])
    def _():
        x1 = xs_ref[:, :D]
        w_a = xs_ref[:, D:D + 1]
        w_b = xs_ref[:, D + 1:D + 2]
        h = _rms(x1, g2_ref[...]).astype(BF16)

        def ffn(wg_ref, wu_ref, w):
            a = jnp.dot(h, wg_ref[...], preferred_element_type=F32)
            b = jnp.dot(h, wu_ref[...], preferred_element_type=F32)
            return (a * _sigmoid(a) * b * w).astype(BF16)

        y = (jnp.dot(ffn(wga_ref, wua_ref, w_a), wda_ref[...], preferred_element_type=F32)
             + jnp.dot(ffn(wgb_ref, wub_ref, w_b), wdb_ref[...], preferred_element_type=F32))
        ys_ref[...] = x1 + y

    @pl.when(pl.program_id(0) >= nused_ref[0])
    def _():
        ys_ref[...] = jnp.zeros_like(ys_ref)


def _pair_experts(tile_ea, tile_eb, tile_blk, n_used, xs, g2, wg, wu, wd, tm):
    rows, width = xs.shape
    E, D, F = wg.shape
    nt = rows // tm
    by_a = lambda i, ea, eb, blk, nu: (ea[i], 0, 0)
    by_b = lambda i, ea, eb, blk, nu: (eb[i], 0, 0)
    tile = lambda i, ea, eb, blk, nu: (blk[i], 0)
    grid_spec = pltpu.PrefetchScalarGridSpec(
        num_scalar_prefetch=4,
        grid=(nt,),
        in_specs=[pl.BlockSpec((tm, width), tile),
                  pl.BlockSpec((1, D), lambda i, ea, eb, blk, nu: (0, 0)),
                  pl.BlockSpec((None, D, F), by_a), pl.BlockSpec((None, D, F), by_b),
                  pl.BlockSpec((None, D, F), by_a), pl.BlockSpec((None, D, F), by_b),
                  pl.BlockSpec((None, F, D), by_a), pl.BlockSpec((None, F, D), by_b)],
        out_specs=pl.BlockSpec((tm, D), lambda i, ea, eb, blk, nu: (i, 0)),
    )
    return pl.pallas_call(
        _pair_experts_body,
        grid_spec=grid_spec,
        out_shape=jax.ShapeDtypeStruct((rows, D), F32),
        compiler_params=_params(1),
        name="moe_pair_experts",
    )(tile_ea, tile_eb, tile_blk, n_used, xs, g2, wg, wg, wu, wu, wd, wd)


def _rot_half_cols(w):
    half = w.shape[-1] // 2
    return jnp.concatenate([-w[..., half:], w[..., :half]], axis=-1)


def _swap_halves(g):
    half = g.shape[-1] // 2
    return jnp.concatenate([g[..., half:], g[..., :half]], axis=-1)


def kernel(x, positions, norm_attn, w_in, b_gate, norm_ckv, w_ukv, q_norm_mla, k_norm_mla, q_norm_dil, k_norm_dil, w_o_mla, w_o_dil, w_out, norm_ffn, w_router_group, b_router_group, w_router_expert, b_router_expert, w_gate, w_up, w_down):
    B, S, D = x.shape
    T = B * S
    depth = norm_attn.shape[0]
    q_cols = MLA_HEADS * MLA_QK
    c0 = q_cols
    c1 = c0 + MLA_KV_RANK
    c2 = c1 + MLA_ROPE
    c3 = c2 + 3 * N_DIL_GROUPS * DIL_GROUP_COLS

    half = MLA_ROPE // 2
    freqs = ROPE_THETA ** (-jnp.arange(half, dtype=F32) / half)
    ang = positions.astype(F32).reshape(T, 1) * freqs
    cos, sin = jnp.cos(ang), jnp.sin(ang)
    cs = jnp.concatenate([jnp.ones((T, MLA_NOPE), F32), cos, cos, sin, sin], axis=-1)

    ex_np = np.zeros((LANES, DIL_GROUP_COLS), np.float32)
    for p in range(DIL_HEADS // 2):
        ex_np[DIL_HEAD_DIM + p, (2 * p) * DIL_HEAD_DIM:(2 * p + 1) * DIL_HEAD_DIM] = 1.0
        ex_np[p, (2 * p + 1) * DIL_HEAD_DIM:(2 * p + 2) * DIL_HEAD_DIM] = 1.0
    ex = jnp.asarray(ex_np, dtype=BF16)
    bd = jnp.asarray(np.kron(np.eye(2, dtype=np.float32), np.ones((DIL_HEAD_DIM, DIL_HEAD_DIM), np.float32)),
                     dtype=BF16)

    xcur = x.reshape(T, D)
    for l in range(depth):
        w = w_in[l]
        wq = w[:, :c0].reshape(D, MLA_HEADS, MLA_QK)
        wq = jnp.concatenate([wq, _rot_half_cols(wq[:, :, MLA_NOPE:])], axis=-1).reshape(D, MLA_HEADS * LANES)
        wkr = w[:, c1:c2]
        wc = jnp.concatenate([w[:, c0:c1], jnp.zeros((D, MLA_NOPE), F32), wkr, _rot_half_cols(wkr)], axis=-1)
        wd = w[:, c2:c3].reshape(D, 3, N_DIL_GROUPS, DIL_GROUP_COLS).transpose(0, 2, 1, 3).reshape(D, c3 - c2)
        wgate = w[:, c3:]

        wukv = w_ukv[l].reshape(MLA_KV_RANK, MLA_HEADS, MLA_NOPE + MLA_V)
        wk = jnp.concatenate([wukv[:, :, :MLA_NOPE], jnp.zeros((MLA_KV_RANK, MLA_HEADS, LANES - MLA_NOPE), F32)], axis=-1)
        wukv2 = jnp.concatenate([wk.reshape(MLA_KV_RANK, MLA_HEADS * LANES),
                                 wukv[:, :, MLA_NOPE:].reshape(MLA_KV_RANK, MLA_HEADS * MLA_V)], axis=-1)

        qn, kn = q_norm_mla[l], k_norm_mla[l]
        scale = MLA_QK ** -0.5 * LOG2E
        gq = (jnp.concatenate([qn, _swap_halves(qn[MLA_NOPE:])]) * scale).reshape(1, LANES)
        gkn = jnp.concatenate([kn[:MLA_NOPE], jnp.zeros((LANES - MLA_NOPE,), F32)]).reshape(1, LANES)
        gkr = jnp.concatenate([jnp.zeros((MLA_NOPE,), F32), kn[MLA_NOPE:], _swap_halves(kn[MLA_NOPE:])]).reshape(1, LANES)

        q1, cc, d0, d12 = _inproj(xcur, norm_attn[l].reshape(1, D), wq.astype(BF16), wc.astype(BF16),
                                  wd.astype(BF16), tm=512)
        qf, kf, v = _mla_prep(q1, cc, cs, gq, gkn, gkr, norm_ckv[l].reshape(1, MLA_KV_RANK),
                              wukv2.astype(BF16), tm=512)
        o_mla = _mla_attn(qf, kf, v, B, S, tq=512)

        ods, ms, ls = [], [], []
        for g in range(N_DIL_GROUPS):
            gq2 = (jnp.tile(q_norm_dil[l, g], 2) * (DIL_HEAD_DIM ** -0.5 * LOG2E)).reshape(1, LANES)
            gk2 = jnp.tile(k_norm_dil[l, g], 2).reshape(1, LANES)
            if g == 0:
                o, m, den = _dil_attn(d0, 0, gq2, gk2, bd, B, S, g, nsub=1)
            else:
                o, m, den = _dil_attn(d12, (g - 1) * 3, gq2, gk2, bd, B, S, g, nsub=1 if g == 1 else 4)
            ods.append(o)
            ms.append(m)
            ls.append(den)

        pad_cols = lambda a: jnp.pad(a, ((0, 0), (0, LANES - a.shape[1])))
        wr = pad_cols(jnp.concatenate([w_router_expert[l], w_router_group[l]], axis=1))
        br = pad_cols(jnp.concatenate([b_router_expert[l], b_router_group[l]]).reshape(1, -1))
        wrh = wr.astype(BF16)
        wrl = (wr - wrh.astype(F32)).astype(BF16)
        x1w, onehot, cnt = _outproj(
            xcur, norm_attn[l].reshape(1, D), wgate.astype(BF16), b_gate[l].reshape(1, 2 * D),
            o_mla, ods, ms, ls, ex, w_o_mla[l].astype(BF16), w_o_dil[l].astype(BF16), w_out[l].astype(BF16),
            norm_ffn[l].reshape(1, D), wrh, wrl, br, tm=512)

        counts = cnt[0].astype(jnp.int32)
        tiles_per = (counts + MOE_TILE - 1) // MOE_TILE
        cum = jnp.cumsum(tiles_per)
        offs = ((cum - tiles_per) * MOE_TILE).astype(F32).reshape(1, LANES)
        n_tiles = T // MOE_TILE + N_CLASSES
        blk = jnp.minimum(jnp.arange(n_tiles, dtype=jnp.int32), cum[-1] - 1)
        tcls = jnp.minimum(jnp.sum((cum[None, :] <= blk[:, None]).astype(jnp.int32), axis=1), N_CLASSES - 1)
        tile_ea = jnp.asarray(_CLASS_EXPERT_A)[tcls]
        tile_eb = jnp.asarray(_CLASS_EXPERT_B)[tcls]

        pos = _positions(onehot, offs, chunk=512)
        xs = _permute_rows(pos, x1w, n_tiles * MOE_TILE, gather=False)
        ys = _pair_experts(tile_ea, tile_eb, blk, cum[-1:].astype(jnp.int32), xs, norm_ffn[l].reshape(1, D),
                           w_gate[l].astype(BF16), w_up[l].astype(BF16), w_down[l].astype(BF16), MOE_TILE)
        xcur = _permute_rows(pos, ys, T, gather=True)
    return xcur.reshape(B, S, D)
```

```python
import functools
import math

import jax
import jax.numpy as jnp
import numpy as np
from jax import lax
from jax.experimental import pallas as pl
from jax.experimental.pallas import tpu as pltpu

F32 = jnp.float32
BF16 = jnp.bfloat16
EPS = 1e-6
NEG = -1e30
LOG2E = math.log2(math.e)

LANES = 128
VMEM_LIMIT = 56 * 1024 * 1024

MLA_HEADS = 8
MLA_NOPE = 64
MLA_ROPE = 32
MLA_QK = MLA_NOPE + MLA_ROPE
MLA_V = 64
MLA_KV_RANK = 256
ROPE_THETA = 10000.0
DIL_PATTERNS = ((128, 1), (512, 4), (2048, 16))
N_DIL_GROUPS = 3
DIL_HEADS = 8
DIL_HEAD_DIM = 64
DIL_GROUP_COLS = DIL_HEADS * DIL_HEAD_DIM
BAND = 128
MLA_LOOKAHEAD = 2
DIL_LOOKAHEAD = 6
DIL_UNROLL = 3
N_EXPERT_GROUPS = 4
EXPERTS_PER_GROUP = 8
N_EXPERTS = N_EXPERT_GROUPS * EXPERTS_PER_GROUP
D_FF_EXPERT = 256
PAIRS_PER_GROUP = EXPERTS_PER_GROUP * (EXPERTS_PER_GROUP - 1) // 2
N_CLASSES = N_EXPERT_GROUPS * PAIRS_PER_GROUP
MOE_TILE = 256


def _class_expert_tables():
    ea, eb = [], []
    for g in range(N_EXPERT_GROUPS):
        for a in range(EXPERTS_PER_GROUP):
            for b in range(a + 1, EXPERTS_PER_GROUP):
                ea.append(g * EXPERTS_PER_GROUP + a)
                eb.append(g * EXPERTS_PER_GROUP + b)
    return np.asarray(ea, np.int32), np.asarray(eb, np.int32)


_CLASS_EXPERT_A, _CLASS_EXPERT_B = _class_expert_tables()


def _params(n_axes):
    return pltpu.CompilerParams(dimension_semantics=("arbitrary",) * n_axes,
                                vmem_limit_bytes=VMEM_LIMIT)


def _const_spec(shape):
    nd = len(shape)
    return pl.BlockSpec(shape, lambda *_: (0,) * nd)


def _rms(x, gain):
    return x * lax.rsqrt(jnp.mean(x * x, axis=-1, keepdims=True) + EPS) * gain


def _sigmoid(x):
    return 1.0 / (1.0 + jnp.exp(-x))


def _split_bf16(v):
    hi = v.astype(BF16)
    lo = (v - hi.astype(F32)).astype(BF16)
    return hi, lo


def _inproj_body(x_ref, g_ref, wq_ref, wc_ref, wd_ref, q_ref, c_ref, d0_ref, d12_ref):
    h = _rms(x_ref[...], g_ref[...]).astype(BF16)
    q_ref[...] = jnp.dot(h, wq_ref[...], preferred_element_type=F32).astype(BF16)
    c_ref[...] = jnp.dot(h, wc_ref[...], preferred_element_type=F32)
    d = jnp.dot(h, wd_ref[...], preferred_element_type=F32)
    n0 = d0_ref.shape[0]
    for j in range(n0):
        d0_ref[j] = d[:, j * LANES:(j + 1) * LANES].astype(BF16)
    for j in range(d12_ref.shape[0]):
        d12_ref[j] = d[:, (n0 + j) * LANES:(n0 + j + 1) * LANES]


def _inproj(x2, g, wq, wc, wd, tm):
    T, D = x2.shape
    nq, nc, nd = wq.shape[1], wc.shape[1], wd.shape[1]
    n0 = 3 * DIL_GROUP_COLS // LANES
    n12 = nd // LANES - n0
    return pl.pallas_call(
        _inproj_body,
        grid=(T // tm,),
        in_specs=[pl.BlockSpec((tm, D), lambda i: (i, 0)),
                  _const_spec((1, D)), _const_spec((D, nq)), _const_spec((D, nc)), _const_spec((D, nd))],
        out_specs=[pl.BlockSpec((tm, nq), lambda i: (i, 0)),
                   pl.BlockSpec((tm, nc), lambda i: (i, 0)),
                   pl.BlockSpec((n0, tm, LANES), lambda i: (0, i, 0)),
                   pl.BlockSpec((n12, tm, LANES), lambda i: (0, i, 0))],
        out_shape=[jax.ShapeDtypeStruct((T, nq), BF16),
                   jax.ShapeDtypeStruct((T, nc), F32),
                   jax.ShapeDtypeStruct((n0, T, LANES), BF16),
                   jax.ShapeDtypeStruct((n12, T, LANES), F32)],
        compiler_params=_params(1),
        name="inproj",
    )(x2, g, wq, wc, wd)


def _mla_prep_body(q_ref, c_ref, cs_ref, gq_ref, gkn_ref, gkr_ref, gc_ref, wukv_ref,
                   qf_ref, kf_ref, v_ref):
    lane = lax.broadcasted_iota(jnp.int32, (1, LANES), 1)
    in_qk = lane < MLA_QK
    mid = jnp.logical_and(lane >= MLA_NOPE, lane < MLA_QK)
    hi = lane >= MLA_QK
    cs = cs_ref[...]
    qmul = cs * gq_ref[...]
    for h in range(MLA_HEADS):
        sl = slice(h * LANES, (h + 1) * LANES)
        qh = q_ref[:, sl].astype(F32)
        ssq = jnp.sum(jnp.where(in_qk, qh * qh, 0.0), axis=-1, keepdims=True)
        r = lax.rsqrt(ssq * (1.0 / MLA_QK) + EPS)
        qf_ref[:, sl] = (qh * qmul * r).astype(BF16)

    ckv = c_ref[:, :MLA_KV_RANK]
    kr = c_ref[:, MLA_KV_RANK:]
    cn = _rms(ckv, gc_ref[...]).astype(BF16)
    kv = jnp.dot(cn, wukv_ref[...], preferred_element_type=F32)
    xr = kr * (cs * gkr_ref[...])
    rk2 = xr + jnp.where(mid, pltpu.roll(xr, 96, 1), jnp.where(hi, pltpu.roll(xr, 32, 1), 0.0))
    ssq_r = jnp.sum(jnp.where(mid, kr * kr, 0.0), axis=-1, keepdims=True)
    gkn = gkn_ref[...]
    for h in range(MLA_HEADS):
        sl = slice(h * LANES, (h + 1) * LANES)
        kn = kv[:, sl]
        ssq = jnp.sum(kn * kn, axis=-1, keepdims=True) + ssq_r
        r = lax.rsqrt(ssq * (1.0 / MLA_QK) + EPS)
        kf_ref[:, sl] = ((kn * gkn + rk2) * r).astype(BF16)
    v_ref[...] = kv[:, MLA_HEADS * LANES:].astype(BF16)


def _mla_prep(q1, c1, cs, gq, gkn, gkr, gc, wukv, tm):
    T = q1.shape[0]
    nq, nc, nkv = q1.shape[1], c1.shape[1], wukv.shape[1]
    nv = MLA_HEADS * MLA_V
    return pl.pallas_call(
        _mla_prep_body,
        grid=(T // tm,),
        in_specs=[pl.BlockSpec((tm, nq), lambda i: (i, 0)),
                  pl.BlockSpec((tm, nc), lambda i: (i, 0)),
                  pl.BlockSpec((tm, LANES), lambda i: (i, 0)),
                  _const_spec((1, LANES)), _const_spec((1, LANES)), _const_spec((1, LANES)),
                  _const_spec((1, MLA_KV_RANK)), _const_spec((MLA_KV_RANK, nkv))],
        out_specs=[pl.BlockSpec((tm, nq), lambda i: (i, 0)),
                   pl.BlockSpec((tm, nq), lambda i: (i, 0)),
                   pl.BlockSpec((tm, nv), lambda i: (i, 0))],
        out_shape=[jax.ShapeDtypeStruct((T, nq), BF16),
                   jax.ShapeDtypeStruct((T, nq), BF16),
                   jax.ShapeDtypeStruct((T, nv), BF16)],
        compiler_params=_params(1),
        name="mla_prep",
    )(q1, c1, cs, gq, gkn, gkr, gc, wukv)


def _mla_attn_body(q_ref, k_ref, v_ref, o_ref, *, tq):
    S = q_ref.shape[0]
    nq = S // tq
    row = lax.broadcasted_iota(jnp.int32, (tq, tq), 0)
    col = lax.broadcasted_iota(jnp.int32, (tq, tq), 1)
    causal = col <= row
    lane = lax.broadcasted_iota(jnp.int32, (1, LANES), 1)
    first = lane < MLA_V
    tasks = [(qi, ki, hh) for qi in range(nq) for ki in range(qi + 1) for hh in range(2)]
    scores, state, done = {}, {}, {}
    for i in range(len(tasks) + MLA_LOOKAHEAD):
        if i < len(tasks):
            qi, ki, hh = tasks[i]
            hs = slice(hh * LANES, (hh + 1) * LANES)
            s = lax.dot_general(q_ref[qi * tq:(qi + 1) * tq, hs], k_ref[ki * tq:(ki + 1) * tq, hs],
                                (((1,), (1,)), ((), ())), preferred_element_type=F32)
            scores[i] = jnp.where(causal, s, NEG) if ki == qi else s
        if i >= MLA_LOOKAHEAD:
            t = i - MLA_LOOKAHEAD
            qi, ki, hh = tasks[t]
            s = scores.pop(t)
            if ki == 0:
                m = jnp.max(s, axis=-1, keepdims=True)
                p = jnp.exp2(s - m)
                l = jnp.sum(p, axis=-1, keepdims=True)
                acc = jnp.dot(p.astype(BF16), v_ref[0:tq, :], preferred_element_type=F32)
            else:
                m_old, l_old, acc_old = state[qi, hh]
                m = jnp.maximum(m_old, jnp.max(s, axis=-1, keepdims=True))
                p = jnp.exp2(s - m)
                alpha = jnp.exp2(m_old - m)
                l = alpha * l_old + jnp.sum(p, axis=-1, keepdims=True)
                acc = alpha * acc_old + jnp.dot(p.astype(BF16), v_ref[ki * tq:(ki + 1) * tq, :],
                                                preferred_element_type=F32)
            state[qi, hh] = (m, l, acc)
            if ki == qi:
                done[qi, hh] = acc * (1.0 / l)
                del state[qi, hh]
                if hh == 1:
                    o_ref[qi * tq:(qi + 1) * tq, :] = jnp.where(
                        first, done.pop((qi, 0)), done.pop((qi, 1))).astype(BF16)


def _mla_attn(qf, kf, v, B, S, tq):
    nq = qf.shape[1]
    pairs = MLA_HEADS // 2
    q3 = qf.reshape(B, S, nq)
    k3 = kf.reshape(B, S, nq)
    v3 = v.reshape(B, S, MLA_HEADS * MLA_V)
    out = pl.pallas_call(
        functools.partial(_mla_attn_body, tq=tq),
        grid=(B, pairs),
        in_specs=[pl.BlockSpec((None, S, 2 * LANES), lambda b, p: (b, 0, p)),
                  pl.BlockSpec((None, S, 2 * LANES), lambda b, p: (b, 0, p)),
                  pl.BlockSpec((None, S, LANES), lambda b, p: (b, 0, p))],
        out_specs=pl.BlockSpec((None, S, LANES), lambda b, p: (b, 0, p)),
        out_shape=jax.ShapeDtypeStruct((B, S, MLA_HEADS * MLA_V), BF16),
        compiler_params=_params(2),
        name="mla_attn",
    )(q3, k3, v3)
    return out.reshape(B * S, MLA_HEADS * MLA_V)


def _dil_attn_body(q_ref, k_ref, v_ref, gq_ref, gk_ref, bd_ref, o_ref, m_ref, l_ref,
                   qn_scr, kn_scr, va_scr, vb_scr, bias_scr, *, dil, L, nsub):
    step = pl.program_id(1)
    lane = lax.broadcasted_iota(jnp.int32, (1, LANES), 1)
    lo = lane < DIL_HEAD_DIM
    bd = bd_ref[...]
    pairs = DIL_HEADS // 2

    def rows(j, start, n):
        if dil == 1:
            return pl.ds(start, n)
        return pl.ds(start * dil + step * nsub + j, n, stride=dil)

    def head_rsqrt(x):
        return lax.rsqrt(jnp.dot((x * x).astype(BF16), bd, preferred_element_type=F32) + EPS)

    for j in range(nsub):
        for pb in range(pairs):
            sl = slice(pb * LANES, (pb + 1) * LANES)
            x = k_ref[pb, rows(j, 0, L), :].astype(F32)
            kn_scr[j, :, sl] = (x * head_rsqrt(x) * gk_ref[...]).astype(BF16)
            x = q_ref[pb, rows(j, 0, L), :].astype(F32)
            xn = x * head_rsqrt(x) * gq_ref[...]
            qn_scr[j, :, (2 * pb) * LANES:(2 * pb + 1) * LANES] = jnp.where(lo, xn, 0.0).astype(BF16)
            qn_scr[j, :, (2 * pb + 1) * LANES:(2 * pb + 2) * LANES] = jnp.where(lo, 0.0, xn).astype(BF16)
            v = v_ref[pb, rows(j, 0, L), :].astype(F32)
            va_scr[j, :, sl] = jnp.where(lo, v, 1.0).astype(BF16)
            vb_scr[j, :, sl] = jnp.where(lo, 1.0, v).astype(BF16)

    @pl.when(jnp.logical_and(pl.program_id(0) == 0, step == 0))
    def _():
        qi = lax.broadcasted_iota(jnp.int32, (BAND, 2 * BAND), 0)
        kj = lax.broadcasted_iota(jnp.int32, (BAND, 2 * BAND), 1)
        delta = qi + BAND - kj
        valid = jnp.logical_and(delta >= 0, delta <= BAND)
        dist = (delta * dil).astype(F32)
        for h in range(DIL_HEADS):
            slope = 2.0 ** (-8.0 * (h + 1) / DIL_HEADS) * LOG2E
            bias_scr[h] = jnp.where(valid, -slope * dist, NEG)

    def run_blocks(blocks):
        tasks = [(bi, pb, hh) for bi in range(len(blocks)) for pb in range(pairs) for hh in range(2)]
        scores = {}
        mrow = [jnp.zeros((BAND, LANES), F32) for _ in blocks]
        lrow = [jnp.ones((BAND, LANES), F32) for _ in blocks]
        acc_first = {}
        for i in range(len(tasks) + DIL_LOOKAHEAD):
            if i < len(tasks):
                bi, pb, hh = tasks[i]
                j, qs, ks, nk = blocks[bi]
                h = 2 * pb + hh
                qh = qn_scr[j, pl.ds(qs, BAND), h * LANES:(h + 1) * LANES]
                kp = kn_scr[j, pl.ds(ks, nk), pb * LANES:(pb + 1) * LANES]
                s = lax.dot_general(qh, kp, (((1,), (1,)), ((), ())), preferred_element_type=F32)
                scores[i] = s + bias_scr[h, :, 2 * BAND - nk:]
            if i >= DIL_LOOKAHEAD:
                t = i - DIL_LOOKAHEAD
                bi, pb, hh = tasks[t]
                j, qs, ks, nk = blocks[bi]
                s = scores.pop(t)
                m = jnp.max(s, axis=-1, keepdims=True)
                p = jnp.exp2(s - m).astype(BF16)
                vaug = (va_scr if hh == 0 else vb_scr)[j, pl.ds(ks, nk), pb * LANES:(pb + 1) * LANES]
                acc = jnp.dot(p, vaug, preferred_element_type=F32)
                pos = DIL_HEAD_DIM + pb if hh == 0 else pb
                mrow[bi] = jnp.where(lane == pos, m, mrow[bi])
                lrow[bi] = jnp.where(lane == pos, acc, lrow[bi])
                if hh == 0:
                    acc_first[bi, pb] = acc
                else:
                    o_ref[pb, rows(j, qs, BAND), :] = jnp.where(
                        lo, acc_first.pop((bi, pb)), acc).astype(o_ref.dtype)
                    if pb == pairs - 1:
                        m_ref[rows(j, qs, BAND), :] = mrow[bi]
                        l_ref[rows(j, qs, BAND), :] = lrow[bi]

    nb = L // BAND
    first = [(j, 0, 0, BAND) for j in range(nsub)]
    rest = nb - 1
    if rest <= DIL_UNROLL:
        run_blocks(first + [(0, b * BAND, (b - 1) * BAND, 2 * BAND) for b in range(1, nb)])
    else:
        run_blocks(first)

        def body(i, carry):
            blocks = []
            for t in range(DIL_UNROLL):
                blk = 1 + DIL_UNROLL * i + t
                blocks.append((0, pl.multiple_of(blk * BAND, BAND), pl.multiple_of((blk - 1) * BAND, BAND),
                               2 * BAND))
            run_blocks(blocks)
            return carry
        lax.fori_loop(0, rest // DIL_UNROLL, body, 0)
        done = 1 + (rest // DIL_UNROLL) * DIL_UNROLL
        if done < nb:
            run_blocks([(0, b * BAND, (b - 1) * BAND, 2 * BAND) for b in range(done, nb)])


def _dil_attn(slabs, first_slab_block, gq2, gk2, bd, B, S, g, nsub):
    window, dil = DIL_PATTERNS[g]
    assert window // dil == BAND
    L = S // dil
    assert L % BAND == 0 and dil % nsub == 0 and (nsub == 1 or L == BAND)
    T = B * S
    pairs = DIL_HEADS // 2

    def in_spec(which):
        return pl.BlockSpec((pairs, S, LANES), lambda b, r: (first_slab_block + which, b, 0))

    stat = pl.BlockSpec((S, LANES), lambda b, r: (b, 0))
    return pl.pallas_call(
        functools.partial(_dil_attn_body, dil=dil, L=L, nsub=nsub),
        grid=(B, dil // nsub),
        in_specs=[in_spec(0), in_spec(1), in_spec(2), _const_spec((1, LANES)), _const_spec((1, LANES)),
                  _const_spec((LANES, LANES))],
        out_specs=[pl.BlockSpec((pairs, S, LANES), lambda b, r: (0, b, 0)), stat, stat],
        out_shape=[jax.ShapeDtypeStruct((pairs, T, LANES), slabs.dtype),
                   jax.ShapeDtypeStruct((T, LANES), F32),
                   jax.ShapeDtypeStruct((T, LANES), F32)],
        scratch_shapes=[pltpu.VMEM((nsub, L, DIL_HEADS * LANES), BF16),
                        pltpu.VMEM((nsub, L, DIL_GROUP_COLS), BF16),
                        pltpu.VMEM((nsub, L, DIL_GROUP_COLS), BF16),
                        pltpu.VMEM((nsub, L, DIL_GROUP_COLS), BF16),
                        pltpu.VMEM((DIL_HEADS, BAND, 2 * BAND), F32)],
        compiler_params=_params(2),
        name=f"dil_attn_g{g}",
    )(slabs, slabs, slabs, gq2, gk2, bd)


def _outproj_body(x_ref, g1_ref, wg_ref, bg_ref, om_ref, od0_ref, od1_ref, od2_ref,
                  m0_ref, m1_ref, m2_ref, l0_ref, l1_ref, l2_ref, ex_ref, wom_ref, wod_ref, wout_ref,
                  g2_ref, wrh_ref, wrl_ref, br_ref, x1_ref, oh_ref, cnt_ref):
    D = x_ref.shape[1]
    x = x_ref[...]
    h = _rms(x, g1_ref[...]).astype(BF16)
    gp = jnp.dot(h, wg_ref[...], preferred_element_type=F32) + bg_ref[...]
    gate_a = _sigmoid(gp[:, :D])
    gate_b = _sigmoid(gp[:, D:])

    m0, m1, m2 = m0_ref[...], m1_ref[...], m2_ref[...]
    mx = jnp.maximum(jnp.maximum(m0, m1), m2)
    e0, e1, e2 = jnp.exp2(m0 - mx), jnp.exp2(m1 - mx), jnp.exp2(m2 - mx)
    inv = 1.0 / (e0 * l0_ref[...] + e1 * l1_ref[...] + e2 * l2_ref[...])
    ex = ex_ref[...]

    def spread(w, od_ref):
        hi, lo = _split_bf16(w)
        wide = (jnp.dot(hi, ex, preferred_element_type=F32) + jnp.dot(lo, ex, preferred_element_type=F32))
        o = jnp.concatenate([od_ref[p] for p in range(od_ref.shape[0])], axis=-1)
        return wide * o.astype(F32)

    od = spread(e0 * inv, od0_ref) + spread(e1 * inv, od1_ref) + spread(e2 * inv, od2_ref)

    a = jnp.dot(om_ref[...], wom_ref[...], preferred_element_type=F32)
    b = jnp.dot(od.astype(BF16), wod_ref[...], preferred_element_type=F32)
    merged = gate_a * a + gate_b * b
    x1 = x + jnp.dot(merged.astype(BF16), wout_ref[...], preferred_element_type=F32)
    x1_ref[:, :D] = x1
    h2 = _rms(x1, g2_ref[...])

    lane = lax.broadcasted_iota(jnp.int32, (1, LANES), 1)
    lane_f = lane.astype(F32)
    big = float(LANES)
    h2h, h2l = _split_bf16(h2)
    wrh = wrh_ref[...]
    logits = (jnp.dot(h2h, wrh, preferred_element_type=F32)
              + jnp.dot(h2h, wrl_ref[...], preferred_element_type=F32)
              + jnp.dot(h2l, wrh, preferred_element_type=F32)) + br_ref[...]
    is_group = jnp.logical_and(lane >= N_EXPERTS, lane < N_EXPERTS + N_EXPERT_GROUPS)
    lg = jnp.where(is_group, logits, -jnp.inf)
    gmax = jnp.max(lg, axis=-1, keepdims=True)
    gi = jnp.min(jnp.where(lg == gmax, lane_f, big), axis=-1, keepdims=True) - float(N_EXPERTS)
    g_p = 1.0 / jnp.sum(jnp.exp(lg - gmax), axis=-1, keepdims=True)
    lo_e = gi * EXPERTS_PER_GROUP
    sel = jnp.logical_and(lane_f >= lo_e, lane_f < lo_e + EXPERTS_PER_GROUP)
    els = jnp.where(sel, logits, -jnp.inf)
    m1 = jnp.max(els, axis=-1, keepdims=True)
    i1 = jnp.min(jnp.where(els == m1, lane_f, big), axis=-1, keepdims=True)
    els2 = jnp.where(lane_f == i1, -jnp.inf, els)
    m2 = jnp.max(els2, axis=-1, keepdims=True)
    i2 = jnp.min(jnp.where(els2 == m2, lane_f, big), axis=-1, keepdims=True)
    t = jnp.exp(m2 - m1)
    w1 = g_p / (1.0 + t)
    w2 = w1 * t
    swap = i2 < i1
    a = jnp.where(swap, i2, i1) - lo_e
    b = jnp.where(swap, i1, i2) - lo_e
    pair = a * (2.0 * EXPERTS_PER_GROUP - 1.0 - a) * 0.5 + (b - a - 1.0)
    cls = gi * float(PAIRS_PER_GROUP) + pair
    x1_ref[:, D:] = jnp.where(lane == 0, jnp.where(swap, w2, w1), jnp.where(lane == 1, jnp.where(swap, w1, w2), 0.0))
    onehot = jnp.where(lane_f == cls, 1.0, 0.0)
    oh_ref[...] = onehot.astype(BF16)

    @pl.when(pl.program_id(0) == 0)
    def _():
        cnt_ref[...] = jnp.zeros_like(cnt_ref)

    cnt_ref[0:1, :] += jnp.sum(onehot, axis=0, keepdims=True)


def _outproj(x2, g1, wg, bg, om, ods, ms, ls, ex, wom, wod, wout, g2, wrh, wrl, br, tm):
    T, D = x2.shape
    row = lambda n: pl.BlockSpec((tm, n), lambda i: (i, 0))
    slab = pl.BlockSpec((DIL_HEADS // 2, tm, LANES), lambda i: (0, i, 0))
    return pl.pallas_call(
        _outproj_body,
        grid=(T // tm,),
        in_specs=[row(D), _const_spec((1, D)), _const_spec(wg.shape), _const_spec((1, 2 * D)),
                  row(om.shape[1]), slab, slab, slab,
                  row(LANES), row(LANES), row(LANES), row(LANES), row(LANES), row(LANES),
                  _const_spec(ex.shape),
                  _const_spec(wom.shape), _const_spec(wod.shape), _const_spec(wout.shape),
                  _const_spec((1, D)), _const_spec(wrh.shape), _const_spec(wrl.shape),
                  _const_spec((1, LANES))],
        out_specs=[row(D + LANES), row(LANES), _const_spec((8, LANES))],
        out_shape=[jax.ShapeDtypeStruct((T, D + LANES), F32),
                   jax.ShapeDtypeStruct((T, LANES), BF16),
                   jax.ShapeDtypeStruct((8, LANES), F32)],
        compiler_params=_params(1),
        name="outproj_router",
    )(x2, g1, wg, bg, om, ods[0], ods[1], ods[2], ms[0], ms[1], ms[2], ls[0], ls[1], ls[2], ex,
      wom, wod, wout, g2, wrh, wrl, br)


def _positions_body(oh_ref, offs_ref, ltri_ref, pos_ref, carry_scr):
    @pl.when(pl.program_id(0) == 0)
    def _():
        carry_scr[...] = jnp.zeros_like(carry_scr)

    oh = oh_ref[...]
    before = jnp.dot(ltri_ref[...], oh, preferred_element_type=F32) + carry_scr[0:1, :]
    val = jnp.where(oh > 0, before + offs_ref[...], 0.0)
    hi = jnp.floor(val * (1.0 / 256.0))
    lo = val - hi * 256.0
    ones = jnp.ones((8, LANES), BF16)
    nt = (((1,), (1,)), ((), ()))
    r = (256.0 * lax.dot_general(ones, hi.astype(BF16), nt, preferred_element_type=F32)
         + lax.dot_general(ones, lo.astype(BF16), nt, preferred_element_type=F32))
    pos_ref[...] = r.astype(jnp.int32)
    carry_scr[...] += jnp.dot(jnp.ones((8, oh.shape[0]), BF16), oh, preferred_element_type=F32)


def _positions(onehot, offs, chunk):
    T = onehot.shape[0]
    ltri = jnp.asarray(np.tril(np.ones((chunk, chunk), np.float32), -1), dtype=BF16)
    pos = pl.pallas_call(
        _positions_body,
        grid=(T // chunk,),
        in_specs=[pl.BlockSpec((chunk, LANES), lambda i: (i, 0)), _const_spec((1, LANES)),
                  _const_spec((chunk, chunk))],
        out_specs=pl.BlockSpec((None, 8, chunk), lambda i: (i, 0, 0)),
        out_shape=jax.ShapeDtypeStruct((T // chunk, 8, chunk), jnp.int32),
        scratch_shapes=[pltpu.VMEM((8, LANES), F32)],
        compiler_params=_params(1),
        name="moe_positions",
    )(onehot, offs, ltri)
    return pos[:, 0, :].reshape(T)


def _permute_rows_body(pos_ref, src_ref, *rest, tm, chunk, gather):
    dst_ref, sems = rest[-2:]

    def row_copy(r, slot):
        p = pos_ref[0, r]
        s_row, d_row = (p, r) if gather else (r, p)
        return pltpu.make_async_copy(src_ref.at[pl.ds(s_row, 1), :], dst_ref.at[pl.ds(d_row, 1), :],
                                     sems.at[slot])

    def start_pair(k, carry, first, slot):
        row_copy(first + 2 * k, slot).start(priority=0)
        row_copy(first + 2 * k + 1, slot).start(priority=1)
        return carry

    def wait(r, carry, slot):
        row_copy(r, slot).wait()
        return carry

    nchunk = tm // chunk
    for c in range(nchunk + 1):
        if c < nchunk:
            lax.fori_loop(0, chunk // 2, functools.partial(start_pair, first=c * chunk, slot=c % 2), 0,
                          unroll=4)
        if c >= 1:
            lax.fori_loop((c - 1) * chunk, c * chunk, functools.partial(wait, slot=(c - 1) % 2), 0, unroll=8)


def _permute_rows(pos, src, dst_rows, gather, tm=1024, chunk=256):
    T = pos.shape[0]
    pos3 = pos.reshape(T // tm, 1, tm)
    width = src.shape[1]
    args = [pos3, src]
    tile = pl.BlockSpec((tm, width), lambda i: (i, 0))
    in_specs = [pl.BlockSpec((None, 1, tm), lambda i: (i, 0, 0), memory_space=pltpu.SMEM)]
    aliases = {}
    if gather:
        in_specs.append(pl.BlockSpec(memory_space=pl.ANY))
        out_spec = tile
    else:
        args.append(jnp.zeros((dst_rows, width), src.dtype))
        in_specs += [tile, pl.BlockSpec(memory_space=pl.ANY)]
        out_spec = pl.BlockSpec(memory_space=pl.ANY)
        aliases = {2: 0}
    return pl.pallas_call(
        functools.partial(_permute_rows_body, tm=tm, chunk=chunk, gather=gather),
        grid=(T // tm,),
        in_specs=in_specs,
        out_specs=out_spec,
        out_shape=jax.ShapeDtypeStruct((dst_rows, width), src.dtype),
        scratch_shapes=[pltpu.SemaphoreType.DMA((2,))],
        input_output_aliases=aliases,
        compiler_params=_params(1),
        name="moe_gather_rows" if gather else "moe_dispatch_rows",
    )(*args)


def _pair_experts_body(ea_ref, eb_ref, blk_ref, nused_ref, xs_ref, g2_ref, wga_ref, wgb_ref, wua_ref, wub_ref,
                       wda_ref, wdb_ref, ys_ref):
    del ea_ref, eb_ref, blk_ref
    D = ys_ref.shape[1]

    @pl.when(pl.program_id(0) < nused_ref[0])
    def _():
        x1 = xs_ref[:, :D]
        w_a = xs_ref[:, D:D + 1]
        w_b = xs_ref[:, D + 1:D + 2]
        h = _rms(x1, g2_ref[...]).astype(BF16)

        def ffn(wg_ref, wu_ref, w):
            a = jnp.dot(h, wg_ref[...], preferred_element_type=F32)
            b = jnp.dot(h, wu_ref[...], preferred_element_type=F32)
            return (a * _sigmoid(a) * b * w).astype(BF16)

        y = (jnp.dot(ffn(wga_ref, wua_ref, w_a), wda_ref[...], preferred_element_type=F32)
             + jnp.dot(ffn(wgb_ref, wub_ref, w_b), wdb_ref[...], preferred_element_type=F32))
        ys_ref[...] = x1 + y

    @pl.when(pl.program_id(0) >= nused_ref[0])
    def _():
        ys_ref[...] = jnp.zeros_like(ys_ref)


def _pair_experts(tile_ea, tile_eb, tile_blk, n_used, xs, g2, wg, wu, wd, tm):
    rows, width = xs.shape
    E, D, F = wg.shape
    nt = rows // tm
    by_a = lambda i, ea, eb, blk, nu: (ea[i], 0, 0)
    by_b = lambda i, ea, eb, blk, nu: (eb[i], 0, 0)
    tile = lambda i, ea, eb, blk, nu: (blk[i], 0)
    grid_spec = pltpu.PrefetchScalarGridSpec(
        num_scalar_prefetch=4,
        grid=(nt,),
        in_specs=[pl.BlockSpec((tm, width), tile),
                  pl.BlockSpec((1, D), lambda i, ea, eb, blk, nu: (0, 0)),
                  pl.BlockSpec((None, D, F), by_a), pl.BlockSpec((None, D, F), by_b),
                  pl.BlockSpec((None, D, F), by_a), pl.BlockSpec((None, D, F), by_b),
                  pl.BlockSpec((None, F, D), by_a), pl.BlockSpec((None, F, D), by_b)],
        out_specs=pl.BlockSpec((tm, D), lambda i, ea, eb, blk, nu: (i, 0)),
    )
    return pl.pallas_call(
        _pair_experts_body,
        grid_spec=grid_spec,
        out_shape=jax.ShapeDtypeStruct((rows, D), F32),
        compiler_params=_params(1),
        name="moe_pair_experts",
    )(tile_ea, tile_eb, tile_blk, n_used, xs, g2, wg, wg, wu, wu, wd, wd)


def _rot_half_cols(w):
    half = w.shape[-1] // 2
    return jnp.concatenate([-w[..., half:], w[..., :half]], axis=-1)


def _swap_halves(g):
    half = g.shape[-1] // 2
    return jnp.concatenate([g[..., half:], g[..., :half]], axis=-1)


def kernel(x, positions, norm_attn, w_in, b_gate, norm_ckv, w_ukv, q_norm_mla, k_norm_mla, q_norm_dil, k_norm_dil, w_o_mla, w_o_dil, w_out, norm_ffn, w_router_group, b_router_group, w_router_expert, b_router_expert, w_gate, w_up, w_down):
    B, S, D = x.shape
    T = B * S
    depth = norm_attn.shape[0]
    q_cols = MLA_HEADS * MLA_QK
    c0 = q_cols
    c1 = c0 + MLA_KV_RANK
    c2 = c1 + MLA_ROPE
    c3 = c2 + 3 * N_DIL_GROUPS * DIL_GROUP_COLS

    half = MLA_ROPE // 2
    freqs = ROPE_THETA ** (-jnp.arange(half, dtype=F32) / half)
    lane_freq = jnp.concatenate([jnp.zeros((MLA_NOPE,), F32), freqs, freqs, freqs, freqs])
    lane_phase = np.concatenate([np.full(MLA_NOPE + MLA_ROPE, np.pi / 2, np.float32), np.zeros(MLA_ROPE, np.float32)])
    cs = jnp.sin(positions.astype(F32).reshape(T, 1) * lane_freq + jnp.asarray(lane_phase))

    ex_np = np.zeros((LANES, DIL_GROUP_COLS), np.float32)
    for p in range(DIL_HEADS // 2):
        ex_np[DIL_HEAD_DIM + p, (2 * p) * DIL_HEAD_DIM:(2 * p + 1) * DIL_HEAD_DIM] = 1.0
        ex_np[p, (2 * p + 1) * DIL_HEAD_DIM:(2 * p + 2) * DIL_HEAD_DIM] = 1.0
    ex = jnp.asarray(ex_np, dtype=BF16)
    bd = jnp.asarray(np.kron(np.eye(2, dtype=np.float32),
                             np.full((DIL_HEAD_DIM, DIL_HEAD_DIM), 1.0 / DIL_HEAD_DIM, np.float32)),
                     dtype=BF16)

    xcur = x.reshape(T, D)
    for l in range(depth):
        w = w_in[l]
        wq = w[:, :c0].reshape(D, MLA_HEADS, MLA_QK)
        wq = jnp.concatenate([wq, _rot_half_cols(wq[:, :, MLA_NOPE:])], axis=-1).reshape(D, MLA_HEADS * LANES)
        wkr = w[:, c1:c2]
        wc = jnp.concatenate([w[:, c0:c1], jnp.zeros((D, MLA_NOPE), F32), wkr, _rot_half_cols(wkr)], axis=-1)
        wd = w[:, c2:c3].reshape(D, 3, N_DIL_GROUPS, DIL_GROUP_COLS).transpose(0, 2, 1, 3).reshape(D, c3 - c2)
        wgate = w[:, c3:]

        wukv = w_ukv[l].reshape(MLA_KV_RANK, MLA_HEADS, MLA_NOPE + MLA_V)
        wk = jnp.concatenate([wukv[:, :, :MLA_NOPE], jnp.zeros((MLA_KV_RANK, MLA_HEADS, LANES - MLA_NOPE), F32)], axis=-1)
        wukv2 = jnp.concatenate([wk.reshape(MLA_KV_RANK, MLA_HEADS * LANES),
                                 wukv[:, :, MLA_NOPE:].reshape(MLA_KV_RANK, MLA_HEADS * MLA_V)], axis=-1)

        qn, kn = q_norm_mla[l], k_norm_mla[l]
        scale = MLA_QK ** -0.5 * LOG2E
        gq = (jnp.concatenate([qn, _swap_halves(qn[MLA_NOPE:])]) * scale).reshape(1, LANES)
        gkn = jnp.concatenate([kn[:MLA_NOPE], jnp.zeros((LANES - MLA_NOPE,), F32)]).reshape(1, LANES)
        gkr = jnp.concatenate([jnp.zeros((MLA_NOPE,), F32), kn[MLA_NOPE:], _swap_halves(kn[MLA_NOPE:])]).reshape(1, LANES)

        q1, cc, d0, d12 = _inproj(xcur, norm_attn[l].reshape(1, D), wq.astype(BF16), wc.astype(BF16),
                                  wd.astype(BF16), tm=512)
        qf, kf, v = _mla_prep(q1, cc, cs, gq, gkn, gkr, norm_ckv[l].reshape(1, MLA_KV_RANK),
                              wukv2.astype(BF16), tm=512)
        o_mla = _mla_attn(qf, kf, v, B, S, tq=512)

        ods, ms, ls = [], [], []
        for g in range(N_DIL_GROUPS):
            gq2 = (jnp.tile(q_norm_dil[l, g], 2) * (DIL_HEAD_DIM ** -0.5 * LOG2E)).reshape(1, LANES)
            gk2 = jnp.tile(k_norm_dil[l, g], 2).reshape(1, LANES)
            if g == 0:
                o, m, den = _dil_attn(d0, 0, gq2, gk2, bd, B, S, g, nsub=1)
            else:
                o, m, den = _dil_attn(d12, (g - 1) * 3, gq2, gk2, bd, B, S, g, nsub=1 if g == 1 else 4)
            ods.append(o)
            ms.append(m)
            ls.append(den)

        pad_cols = lambda a: jnp.pad(a, ((0, 0), (0, LANES - a.shape[1])))
        wr = pad_cols(jnp.concatenate([w_router_expert[l], w_router_group[l]], axis=1))
        br = pad_cols(jnp.concatenate([b_router_expert[l], b_router_group[l]]).reshape(1, -1))
        wrh = wr.astype(BF16)
        wrl = (wr - wrh.astype(F32)).astype(BF16)
        x1w, onehot, cnt = _outproj(
            xcur, norm_attn[l].reshape(1, D), wgate.astype(BF16), b_gate[l].reshape(1, 2 * D),
            o_mla, ods, ms, ls, ex, w_o_mla[l].astype(BF16), w_o_dil[l].astype(BF16), w_out[l].astype(BF16),
            norm_ffn[l].reshape(1, D), wrh, wrl, br, tm=512)

        counts = cnt[0].astype(jnp.int32)
        tiles_per = (counts + MOE_TILE - 1) // MOE_TILE
        cum = jnp.cumsum(tiles_per)
        offs = ((cum - tiles_per) * MOE_TILE).astype(F32).reshape(1, LANES)
        n_tiles = T // MOE_TILE + N_CLASSES
        blk = jnp.minimum(jnp.arange(n_tiles, dtype=jnp.int32), cum[-1] - 1)
        tcls = jnp.minimum(jnp.sum((cum[None, :] <= blk[:, None]).astype(jnp.int32), axis=1), N_CLASSES - 1)
        tile_ea = jnp.asarray(_CLASS_EXPERT_A)[tcls]
        tile_eb = jnp.asarray(_CLASS_EXPERT_B)[tcls]

        pos = _positions(onehot, offs, chunk=512)
        xs = _permute_rows(pos, x1w, n_tiles * MOE_TILE, gather=False)
        ys = _pair_experts(tile_ea, tile_eb, blk, cum[-1:].astype(jnp.int32), xs, norm_ffn[l].reshape(1, D),
                           w_gate[l].astype(BF16), w_up[l].astype(BF16), w_down[l].astype(BF16), MOE_TILE)
        xcur = _permute_rows(pos, ys, T, gather=True)
    return xcur.reshape(B, S, D)
```

```python
import functools
import math

import jax
import jax.numpy as jnp
import numpy as np
from jax import lax
from jax.experimental import pallas as pl
from jax.experimental.pallas import tpu as pltpu

F32 = jnp.float32
BF16 = jnp.bfloat16
EPS = 1e-6
NEG = -1e30
LOG2E = math.log2(math.e)

LANES = 128
VMEM_LIMIT = 56 * 1024 * 1024

MLA_HEADS = 8
MLA_NOPE = 64
MLA_ROPE = 32
MLA_QK = MLA_NOPE + MLA_ROPE
MLA_V = 64
MLA_KV_RANK = 256
ROPE_THETA = 10000.0
DIL_PATTERNS = ((128, 1), (512, 4), (2048, 16))
N_DIL_GROUPS = 3
DIL_HEADS = 8
DIL_HEAD_DIM = 64
DIL_GROUP_COLS = DIL_HEADS * DIL_HEAD_DIM
BAND = 128
MLA_LOOKAHEAD = 2
DIL_LOOKAHEAD = 6
DIL_UNROLL = 3
N_EXPERT_GROUPS = 4
EXPERTS_PER_GROUP = 8
N_EXPERTS = N_EXPERT_GROUPS * EXPERTS_PER_GROUP
D_FF_EXPERT = 256
PAIRS_PER_GROUP = EXPERTS_PER_GROUP * (EXPERTS_PER_GROUP - 1) // 2
N_CLASSES = N_EXPERT_GROUPS * PAIRS_PER_GROUP
MOE_TILE = 256


def _class_expert_tables():
    ea, eb = [], []
    for g in range(N_EXPERT_GROUPS):
        for a in range(EXPERTS_PER_GROUP):
            for b in range(a + 1, EXPERTS_PER_GROUP):
                ea.append(g * EXPERTS_PER_GROUP + a)
                eb.append(g * EXPERTS_PER_GROUP + b)
    return np.asarray(ea, np.int32), np.asarray(eb, np.int32)


_CLASS_EXPERT_A, _CLASS_EXPERT_B = _class_expert_tables()


def _params(n_axes):
    return pltpu.CompilerParams(dimension_semantics=("arbitrary",) * n_axes,
                                vmem_limit_bytes=VMEM_LIMIT)


def _const_spec(shape):
    nd = len(shape)
    return pl.BlockSpec(shape, lambda *_: (0,) * nd)


def _rms(x, gain):
    return x * lax.rsqrt(jnp.mean(x * x, axis=-1, keepdims=True) + EPS) * gain


def _sigmoid(x):
    return 1.0 / (1.0 + jnp.exp(-x))


def _split_bf16(v):
    hi = v.astype(BF16)
    lo = (v - hi.astype(F32)).astype(BF16)
    return hi, lo


def _inproj_body(x_ref, g_ref, wq_ref, wc_ref, wd_ref, q_ref, c_ref, d0_ref, d12_ref):
    h = _rms(x_ref[...], g_ref[...]).astype(BF16)
    q_ref[...] = jnp.dot(h, wq_ref[...], preferred_element_type=F32).astype(BF16)
    c_ref[...] = jnp.dot(h, wc_ref[...], preferred_element_type=F32)
    d = jnp.dot(h, wd_ref[...], preferred_element_type=F32)
    n0 = d0_ref.shape[0]
    for j in range(n0):
        d0_ref[j] = d[:, j * LANES:(j + 1) * LANES].astype(BF16)
    for j in range(d12_ref.shape[0]):
        d12_ref[j] = d[:, (n0 + j) * LANES:(n0 + j + 1) * LANES]


def _inproj(x2, g, wq, wc, wd, tm):
    T, D = x2.shape
    nq, nc, nd = wq.shape[1], wc.shape[1], wd.shape[1]
    n0 = 3 * DIL_GROUP_COLS // LANES
    n12 = nd // LANES - n0
    return pl.pallas_call(
        _inproj_body,
        grid=(T // tm,),
        in_specs=[pl.BlockSpec((tm, D), lambda i: (i, 0)),
                  _const_spec((1, D)), _const_spec((D, nq)), _const_spec((D, nc)), _const_spec((D, nd))],
        out_specs=[pl.BlockSpec((tm, nq), lambda i: (i, 0)),
                   pl.BlockSpec((tm, nc), lambda i: (i, 0)),
                   pl.BlockSpec((n0, tm, LANES), lambda i: (0, i, 0)),
                   pl.BlockSpec((n12, tm, LANES), lambda i: (0, i, 0))],
        out_shape=[jax.ShapeDtypeStruct((T, nq), BF16),
                   jax.ShapeDtypeStruct((T, nc), F32),
                   jax.ShapeDtypeStruct((n0, T, LANES), BF16),
                   jax.ShapeDtypeStruct((n12, T, LANES), F32)],
        compiler_params=_params(1),
        name="inproj",
    )(x2, g, wq, wc, wd)


def _mla_prep_body(q_ref, c_ref, cs_ref, gq_ref, gkn_ref, gkr_ref, gc_ref, wukv_ref,
                   qf_ref, kf_ref, v_ref):
    lane = lax.broadcasted_iota(jnp.int32, (1, LANES), 1)
    in_qk = lane < MLA_QK
    mid = jnp.logical_and(lane >= MLA_NOPE, lane < MLA_QK)
    hi = lane >= MLA_QK
    cs = cs_ref[...]
    qmul = cs * gq_ref[...]
    for h in range(MLA_HEADS):
        sl = slice(h * LANES, (h + 1) * LANES)
        qh = q_ref[:, sl].astype(F32)
        ssq = jnp.sum(jnp.where(in_qk, qh * qh, 0.0), axis=-1, keepdims=True)
        r = lax.rsqrt(ssq * (1.0 / MLA_QK) + EPS)
        qf_ref[:, sl] = (qh * qmul * r).astype(BF16)

    ckv = c_ref[:, :MLA_KV_RANK]
    kr = c_ref[:, MLA_KV_RANK:]
    cn = _rms(ckv, gc_ref[...]).astype(BF16)
    kv = jnp.dot(cn, wukv_ref[...], preferred_element_type=F32)
    xr = kr * (cs * gkr_ref[...])
    rk2 = xr + jnp.where(mid, pltpu.roll(xr, 96, 1), jnp.where(hi, pltpu.roll(xr, 32, 1), 0.0))
    ssq_r = jnp.sum(jnp.where(mid, kr * kr, 0.0), axis=-1, keepdims=True)
    gkn = gkn_ref[...]
    for h in range(MLA_HEADS):
        sl = slice(h * LANES, (h + 1) * LANES)
        kn = kv[:, sl]
        ssq = jnp.sum(kn * kn, axis=-1, keepdims=True) + ssq_r
        r = lax.rsqrt(ssq * (1.0 / MLA_QK) + EPS)
        kf_ref[:, sl] = ((kn * gkn + rk2) * r).astype(BF16)
    v_ref[...] = kv[:, MLA_HEADS * LANES:].astype(BF16)


def _mla_prep(q1, c1, cs, gq, gkn, gkr, gc, wukv, tm):
    T = q1.shape[0]
    nq, nc, nkv = q1.shape[1], c1.shape[1], wukv.shape[1]
    nv = MLA_HEADS * MLA_V
    return pl.pallas_call(
        _mla_prep_body,
        grid=(T // tm,),
        in_specs=[pl.BlockSpec((tm, nq), lambda i: (i, 0)),
                  pl.BlockSpec((tm, nc), lambda i: (i, 0)),
                  pl.BlockSpec((tm, LANES), lambda i: (i, 0)),
                  _const_spec((1, LANES)), _const_spec((1, LANES)), _const_spec((1, LANES)),
                  _const_spec((1, MLA_KV_RANK)), _const_spec((MLA_KV_RANK, nkv))],
        out_specs=[pl.BlockSpec((tm, nq), lambda i: (i, 0)),
                   pl.BlockSpec((tm, nq), lambda i: (i, 0)),
                   pl.BlockSpec((tm, nv), lambda i: (i, 0))],
        out_shape=[jax.ShapeDtypeStruct((T, nq), BF16),
                   jax.ShapeDtypeStruct((T, nq), BF16),
                   jax.ShapeDtypeStruct((T, nv), BF16)],
        compiler_params=_params(1),
        name="mla_prep",
    )(q1, c1, cs, gq, gkn, gkr, gc, wukv)


def _mla_attn_body(q_ref, k_ref, v_ref, o_ref, *, tq):
    S = q_ref.shape[0]
    nq = S // tq
    row = lax.broadcasted_iota(jnp.int32, (tq, tq), 0)
    col = lax.broadcasted_iota(jnp.int32, (tq, tq), 1)
    causal = col <= row
    lane = lax.broadcasted_iota(jnp.int32, (1, LANES), 1)
    first = lane < MLA_V
    tasks = [(qi, ki, hh) for qi in range(nq) for ki in range(qi + 1) for hh in range(2)]
    scores, state, done = {}, {}, {}
    for i in range(len(tasks) + MLA_LOOKAHEAD):
        if i < len(tasks):
            qi, ki, hh = tasks[i]
            hs = slice(hh * LANES, (hh + 1) * LANES)
            s = lax.dot_general(q_ref[qi * tq:(qi + 1) * tq, hs], k_ref[ki * tq:(ki + 1) * tq, hs],
                                (((1,), (1,)), ((), ())), preferred_element_type=F32)
            scores[i] = jnp.where(causal, s, NEG) if ki == qi else s
        if i >= MLA_LOOKAHEAD:
            t = i - MLA_LOOKAHEAD
            qi, ki, hh = tasks[t]
            s = scores.pop(t)
            if ki == 0:
                m = jnp.max(s, axis=-1, keepdims=True)
                p = jnp.exp2(s - m)
                l = jnp.sum(p, axis=-1, keepdims=True)
                acc = jnp.dot(p.astype(BF16), v_ref[0:tq, :], preferred_element_type=F32)
            else:
                m_old, l_old, acc_old = state[qi, hh]
                m = jnp.maximum(m_old, jnp.max(s, axis=-1, keepdims=True))
                p = jnp.exp2(s - m)
                alpha = jnp.exp2(m_old - m)
                l = alpha * l_old + jnp.sum(p, axis=-1, keepdims=True)
                acc = alpha * acc_old + jnp.dot(p.astype(BF16), v_ref[ki * tq:(ki + 1) * tq, :],
                                                preferred_element_type=F32)
            state[qi, hh] = (m, l, acc)
            if ki == qi:
                done[qi, hh] = acc * (1.0 / l)
                del state[qi, hh]
                if hh == 1:
                    o_ref[qi * tq:(qi + 1) * tq, :] = jnp.where(
                        first, done.pop((qi, 0)), done.pop((qi, 1))).astype(BF16)


def _mla_attn(qf, kf, v, B, S, tq):
    nq = qf.shape[1]
    pairs = MLA_HEADS // 2
    q3 = qf.reshape(B, S, nq)
    k3 = kf.reshape(B, S, nq)
    v3 = v.reshape(B, S, MLA_HEADS * MLA_V)
    out = pl.pallas_call(
        functools.partial(_mla_attn_body, tq=tq),
        grid=(B, pairs),
        in_specs=[pl.BlockSpec((None, S, 2 * LANES), lambda b, p: (b, 0, p)),
                  pl.BlockSpec((None, S, 2 * LANES), lambda b, p: (b, 0, p)),
                  pl.BlockSpec((None, S, LANES), lambda b, p: (b, 0, p))],
        out_specs=pl.BlockSpec((None, S, LANES), lambda b, p: (b, 0, p)),
        out_shape=jax.ShapeDtypeStruct((B, S, MLA_HEADS * MLA_V), BF16),
        compiler_params=_params(2),
        name="mla_attn",
    )(q3, k3, v3)
    return out.reshape(B * S, MLA_HEADS * MLA_V)


def _dil_attn_body(q_ref, k_ref, v_ref, gq_ref, gk_ref, bd_ref, o_ref, m_ref, l_ref,
                   qn_scr, kn_scr, va_scr, vb_scr, bias_scr, *, dil, L, nsub):
    step = pl.program_id(1)
    lane = lax.broadcasted_iota(jnp.int32, (1, LANES), 1)
    lo = lane < DIL_HEAD_DIM
    bd = bd_ref[...]
    pairs = DIL_HEADS // 2

    def rows(j, start, n):
        if dil == 1:
            return pl.ds(start, n)
        return pl.ds(start * dil + step * nsub + j, n, stride=dil)

    def head_rsqrt(x):
        return lax.rsqrt(jnp.dot((x * x).astype(BF16), bd, preferred_element_type=F32) + EPS)

    for j in range(nsub):
        for pb in range(pairs):
            sl = slice(pb * LANES, (pb + 1) * LANES)
            x = k_ref[pb, rows(j, 0, L), :].astype(F32)
            kn_scr[j, :, sl] = (x * head_rsqrt(x) * gk_ref[...]).astype(BF16)
            x = q_ref[pb, rows(j, 0, L), :].astype(F32)
            xn = x * head_rsqrt(x) * gq_ref[...]
            qn_scr[j, :, (2 * pb) * LANES:(2 * pb + 1) * LANES] = jnp.where(lo, xn, 0.0).astype(BF16)
            qn_scr[j, :, (2 * pb + 1) * LANES:(2 * pb + 2) * LANES] = jnp.where(lo, 0.0, xn).astype(BF16)
            v = v_ref[pb, rows(j, 0, L), :].astype(F32)
            va_scr[j, :, sl] = jnp.where(lo, v, 1.0).astype(BF16)
            vb_scr[j, :, sl] = jnp.where(lo, 1.0, v).astype(BF16)

    @pl.when(jnp.logical_and(pl.program_id(0) == 0, step == 0))
    def _():
        qi = lax.broadcasted_iota(jnp.int32, (BAND, 2 * BAND), 0)
        kj = lax.broadcasted_iota(jnp.int32, (BAND, 2 * BAND), 1)
        delta = qi + BAND - kj
        valid = jnp.logical_and(delta >= 0, delta <= BAND)
        dist = (delta * dil).astype(F32)
        for h in range(DIL_HEADS):
            slope = 2.0 ** (-8.0 * (h + 1) / DIL_HEADS) * LOG2E
            bias_scr[h] = jnp.where(valid, -slope * dist, NEG)

    def run_blocks(blocks):
        tasks = [(bi, pb, hh) for bi in range(len(blocks)) for pb in range(pairs) for hh in range(2)]
        scores = {}
        mrow = [jnp.zeros((BAND, LANES), F32) for _ in blocks]
        lrow = [jnp.ones((BAND, LANES), F32) for _ in blocks]
        acc_first = {}
        for i in range(len(tasks) + DIL_LOOKAHEAD):
            if i < len(tasks):
                bi, pb, hh = tasks[i]
                j, qs, ks, nk = blocks[bi]
                h = 2 * pb + hh
                qh = qn_scr[j, pl.ds(qs, BAND), h * LANES:(h + 1) * LANES]
                kp = kn_scr[j, pl.ds(ks, nk), pb * LANES:(pb + 1) * LANES]
                s = lax.dot_general(qh, kp, (((1,), (1,)), ((), ())), preferred_element_type=F32)
                scores[i] = s + bias_scr[h, :, 2 * BAND - nk:]
            if i >= DIL_LOOKAHEAD:
                t = i - DIL_LOOKAHEAD
                bi, pb, hh = tasks[t]
                j, qs, ks, nk = blocks[bi]
                s = scores.pop(t)
                m = jnp.max(s, axis=-1, keepdims=True)
                p = jnp.exp2(s - m).astype(BF16)
                vaug = (va_scr if hh == 0 else vb_scr)[j, pl.ds(ks, nk), pb * LANES:(pb + 1) * LANES]
                acc = jnp.dot(p, vaug, preferred_element_type=F32)
                pos = DIL_HEAD_DIM + pb if hh == 0 else pb
                mrow[bi] = jnp.where(lane == pos, m, mrow[bi])
                lrow[bi] = jnp.where(lane == pos, acc, lrow[bi])
                if hh == 0:
                    acc_first[bi, pb] = acc
                else:
                    o_ref[pb, rows(j, qs, BAND), :] = jnp.where(
                        lo, acc_first.pop((bi, pb)), acc).astype(o_ref.dtype)
                    if pb == pairs - 1:
                        m_ref[rows(j, qs, BAND), :] = mrow[bi]
                        l_ref[rows(j, qs, BAND), :] = lrow[bi]

    nb = L // BAND
    first = [(j, 0, 0, BAND) for j in range(nsub)]
    rest = nb - 1
    if rest <= DIL_UNROLL:
        run_blocks(first + [(0, b * BAND, (b - 1) * BAND, 2 * BAND) for b in range(1, nb)])
    else:
        run_blocks(first)

        def body(i, carry):
            blocks = []
            for t in range(DIL_UNROLL):
                blk = 1 + DIL_UNROLL * i + t
                blocks.append((0, pl.multiple_of(blk * BAND, BAND), pl.multiple_of((blk - 1) * BAND, BAND),
                               2 * BAND))
            run_blocks(blocks)
            return carry
        lax.fori_loop(0, rest // DIL_UNROLL, body, 0)
        done = 1 + (rest // DIL_UNROLL) * DIL_UNROLL
        if done < nb:
            run_blocks([(0, b * BAND, (b - 1) * BAND, 2 * BAND) for b in range(done, nb)])


def _dil_attn(slabs, first_slab_block, gq2, gk2, bd, B, S, g, nsub):
    window, dil = DIL_PATTERNS[g]
    assert window // dil == BAND
    L = S // dil
    assert L % BAND == 0 and dil % nsub == 0 and (nsub == 1 or L == BAND)
    T = B * S
    pairs = DIL_HEADS // 2

    def in_spec(which):
        return pl.BlockSpec((pairs, S, LANES), lambda b, r: (first_slab_block + which, b, 0))

    stat = pl.BlockSpec((S, LANES), lambda b, r: (b, 0))
    return pl.pallas_call(
        functools.partial(_dil_attn_body, dil=dil, L=L, nsub=nsub),
        grid=(B, dil // nsub),
        in_specs=[in_spec(0), in_spec(1), in_spec(2), _const_spec((1, LANES)), _const_spec((1, LANES)),
                  _const_spec((LANES, LANES))],
        out_specs=[pl.BlockSpec((pairs, S, LANES), lambda b, r: (0, b, 0)), stat, stat],
        out_shape=[jax.ShapeDtypeStruct((pairs, T, LANES), slabs.dtype),
                   jax.ShapeDtypeStruct((T, LANES), F32),
                   jax.ShapeDtypeStruct((T, LANES), F32)],
        scratch_shapes=[pltpu.VMEM((nsub, L, DIL_HEADS * LANES), BF16),
                        pltpu.VMEM((nsub, L, DIL_GROUP_COLS), BF16),
                        pltpu.VMEM((nsub, L, DIL_GROUP_COLS), BF16),
                        pltpu.VMEM((nsub, L, DIL_GROUP_COLS), BF16),
                        pltpu.VMEM((DIL_HEADS, BAND, 2 * BAND), F32)],
        compiler_params=_params(2),
        name=f"dil_attn_g{g}",
    )(slabs, slabs, slabs, gq2, gk2, bd)


def _outproj_body(x_ref, g1_ref, wg_ref, bg_ref, om_ref, od0_ref, od1_ref, od2_ref,
                  m0_ref, m1_ref, m2_ref, l0_ref, l1_ref, l2_ref, ex_ref, wom_ref, wod_ref, wout_ref,
                  g2_ref, wrh_ref, wrl_ref, br_ref, x1_ref, oh_ref, cnt_ref):
    D = x_ref.shape[1]
    x = x_ref[...]
    h = _rms(x, g1_ref[...]).astype(BF16)
    gp = jnp.dot(h, wg_ref[...], preferred_element_type=F32) + bg_ref[...]
    gate_a = _sigmoid(gp[:, :D])
    gate_b = _sigmoid(gp[:, D:])

    m0, m1, m2 = m0_ref[...], m1_ref[...], m2_ref[...]
    mx = jnp.maximum(jnp.maximum(m0, m1), m2)
    e0, e1, e2 = jnp.exp2(m0 - mx), jnp.exp2(m1 - mx), jnp.exp2(m2 - mx)
    inv = 1.0 / (e0 * l0_ref[...] + e1 * l1_ref[...] + e2 * l2_ref[...])
    ex = ex_ref[...]

    def spread(w, od_ref):
        wide = jnp.dot(jnp.concatenate(_split_bf16(w), axis=-1), ex, preferred_element_type=F32)
        o = jnp.concatenate([od_ref[p] for p in range(od_ref.shape[0])], axis=-1)
        return wide * o.astype(F32)

    od = spread(e0 * inv, od0_ref) + spread(e1 * inv, od1_ref) + spread(e2 * inv, od2_ref)

    a = jnp.dot(om_ref[...], wom_ref[...], preferred_element_type=F32)
    b = jnp.dot(od.astype(BF16), wod_ref[...], preferred_element_type=F32)
    merged = gate_a * a + gate_b * b
    x1 = x + jnp.dot(merged.astype(BF16), wout_ref[...], preferred_element_type=F32)
    x1_ref[:, :D] = x1
    h2 = _rms(x1, g2_ref[...])

    lane = lax.broadcasted_iota(jnp.int32, (1, LANES), 1)
    lane_f = lane.astype(F32)
    big = float(LANES)
    h2h, h2l = _split_bf16(h2)
    both = jnp.dot(h2h, wrl_ref[...], preferred_element_type=F32)
    logits = (both[:, :LANES] + both[:, LANES:]
              + jnp.dot(h2l, wrh_ref[...], preferred_element_type=F32)) + br_ref[...]
    is_group = jnp.logical_and(lane >= N_EXPERTS, lane < N_EXPERTS + N_EXPERT_GROUPS)
    lg = jnp.where(is_group, logits, -jnp.inf)
    gmax = jnp.max(lg, axis=-1, keepdims=True)
    gi = jnp.min(jnp.where(lg == gmax, lane_f, big), axis=-1, keepdims=True) - float(N_EXPERTS)
    g_p = 1.0 / jnp.sum(jnp.exp(lg - gmax), axis=-1, keepdims=True)
    lo_e = gi * EXPERTS_PER_GROUP
    sel = jnp.logical_and(lane_f >= lo_e, lane_f < lo_e + EXPERTS_PER_GROUP)
    els = jnp.where(sel, logits, -jnp.inf)
    m1 = jnp.max(els, axis=-1, keepdims=True)
    i1 = jnp.min(jnp.where(els == m1, lane_f, big), axis=-1, keepdims=True)
    els2 = jnp.where(lane_f == i1, -jnp.inf, els)
    m2 = jnp.max(els2, axis=-1, keepdims=True)
    i2 = jnp.min(jnp.where(els2 == m2, lane_f, big), axis=-1, keepdims=True)
    t = jnp.exp(m2 - m1)
    w1 = g_p / (1.0 + t)
    w2 = w1 * t
    swap = i2 < i1
    a = jnp.where(swap, i2, i1) - lo_e
    b = jnp.where(swap, i1, i2) - lo_e
    pair = a * (2.0 * EXPERTS_PER_GROUP - 1.0 - a) * 0.5 + (b - a - 1.0)
    cls = gi * float(PAIRS_PER_GROUP) + pair
    x1_ref[:, D:] = jnp.where(lane == 0, jnp.where(swap, w2, w1), jnp.where(lane == 1, jnp.where(swap, w1, w2), 0.0))
    onehot = jnp.where(lane_f == cls, 1.0, 0.0)
    oh_ref[...] = onehot.astype(BF16)

    @pl.when(pl.program_id(0) == 0)
    def _():
        cnt_ref[...] = jnp.zeros_like(cnt_ref)

    cnt_ref[0:1, :] += jnp.sum(onehot, axis=0, keepdims=True)


def _outproj(x2, g1, wg, bg, om, ods, ms, ls, ex, wom, wod, wout, g2, wrh, wrl, br, tm):
    T, D = x2.shape
    row = lambda n: pl.BlockSpec((tm, n), lambda i: (i, 0))
    slab = pl.BlockSpec((DIL_HEADS // 2, tm, LANES), lambda i: (0, i, 0))
    return pl.pallas_call(
        _outproj_body,
        grid=(T // tm,),
        in_specs=[row(D), _const_spec((1, D)), _const_spec(wg.shape), _const_spec((1, 2 * D)),
                  row(om.shape[1]), slab, slab, slab,
                  row(LANES), row(LANES), row(LANES), row(LANES), row(LANES), row(LANES),
                  _const_spec(ex.shape),
                  _const_spec(wom.shape), _const_spec(wod.shape), _const_spec(wout.shape),
                  _const_spec((1, D)), _const_spec(wrh.shape), _const_spec(wrl.shape),
                  _const_spec((1, LANES))],
        out_specs=[row(D + LANES), row(LANES), _const_spec((8, LANES))],
        out_shape=[jax.ShapeDtypeStruct((T, D + LANES), F32),
                   jax.ShapeDtypeStruct((T, LANES), BF16),
                   jax.ShapeDtypeStruct((8, LANES), F32)],
        compiler_params=_params(1),
        name="outproj_router",
    )(x2, g1, wg, bg, om, ods[0], ods[1], ods[2], ms[0], ms[1], ms[2], ls[0], ls[1], ls[2], ex,
      wom, wod, wout, g2, wrh, wrl, br)


def _positions_body(oh_ref, offs_ref, ltri_ref, pos_ref, carry_scr):
    @pl.when(pl.program_id(0) == 0)
    def _():
        carry_scr[...] = jnp.zeros_like(carry_scr)

    oh = oh_ref[...]
    before = jnp.dot(ltri_ref[...], oh, preferred_element_type=F32) + carry_scr[0:1, :]
    val = jnp.where(oh > 0, before + offs_ref[...], 0.0)
    hi = jnp.floor(val * (1.0 / 256.0))
    lo = val - hi * 256.0
    ones = jnp.ones((8, LANES), BF16)
    nt = (((1,), (1,)), ((), ()))
    r = (256.0 * lax.dot_general(ones, hi.astype(BF16), nt, preferred_element_type=F32)
         + lax.dot_general(ones, lo.astype(BF16), nt, preferred_element_type=F32))
    pos_ref[...] = r.astype(jnp.int32)
    carry_scr[...] += jnp.dot(jnp.ones((8, oh.shape[0]), BF16), oh, preferred_element_type=F32)


def _positions(onehot, offs, chunk):
    T = onehot.shape[0]
    ltri = jnp.asarray(np.tril(np.ones((chunk, chunk), np.float32), -1), dtype=BF16)
    pos = pl.pallas_call(
        _positions_body,
        grid=(T // chunk,),
        in_specs=[pl.BlockSpec((chunk, LANES), lambda i: (i, 0)), _const_spec((1, LANES)),
                  _const_spec((chunk, chunk))],
        out_specs=pl.BlockSpec((None, 8, chunk), lambda i: (i, 0, 0)),
        out_shape=jax.ShapeDtypeStruct((T // chunk, 8, chunk), jnp.int32),
        scratch_shapes=[pltpu.VMEM((8, LANES), F32)],
        compiler_params=_params(1),
        name="moe_positions",
    )(onehot, offs, ltri)
    return pos[:, 0, :].reshape(T)


def _permute_rows_body(pos_ref, src_ref, *rest, tm, chunk, gather):
    dst_ref, sems = rest[-2:]

    def row_copy(r, slot):
        p = pos_ref[0, r]
        s_row, d_row = (p, r) if gather else (r, p)
        return pltpu.make_async_copy(src_ref.at[pl.ds(s_row, 1), :], dst_ref.at[pl.ds(d_row, 1), :],
                                     sems.at[slot])

    def start_pair(k, carry, first, slot):
        row_copy(first + 2 * k, slot).start(priority=0)
        row_copy(first + 2 * k + 1, slot).start(priority=1)
        return carry

    def wait(r, carry, slot):
        row_copy(r, slot).wait()
        return carry

    nchunk = tm // chunk
    for c in range(nchunk + 1):
        if c < nchunk:
            lax.fori_loop(0, chunk // 2, functools.partial(start_pair, first=c * chunk, slot=c % 2), 0,
                          unroll=4)
        if c >= 1:
            lax.fori_loop((c - 1) * chunk, c * chunk, functools.partial(wait, slot=(c - 1) % 2), 0, unroll=8)


def _permute_rows(pos, src, dst_rows, gather, tm=1024, chunk=256):
    T = pos.shape[0]
    pos3 = pos.reshape(T // tm, 1, tm)
    width = src.shape[1]
    args = [pos3, src]
    tile = pl.BlockSpec((tm, width), lambda i: (i, 0))
    in_specs = [pl.BlockSpec((None, 1, tm), lambda i: (i, 0, 0), memory_space=pltpu.SMEM)]
    aliases = {}
    if gather:
        in_specs.append(pl.BlockSpec(memory_space=pl.ANY))
        out_spec = tile
    else:
        args.append(jnp.zeros((dst_rows, width), src.dtype))
        in_specs += [tile, pl.BlockSpec(memory_space=pl.ANY)]
        out_spec = pl.BlockSpec(memory_space=pl.ANY)
        aliases = {2: 0}
    return pl.pallas_call(
        functools.partial(_permute_rows_body, tm=tm, chunk=chunk, gather=gather),
        grid=(T // tm,),
        in_specs=in_specs,
        out_specs=out_spec,
        out_shape=jax.ShapeDtypeStruct((dst_rows, width), src.dtype),
        scratch_shapes=[pltpu.SemaphoreType.DMA((2,))],
        input_output_aliases=aliases,
        compiler_params=_params(1),
        name="moe_gather_rows" if gather else "moe_dispatch_rows",
    )(*args)


def _pair_experts_body(ea_ref, eb_ref, blk_ref, nused_ref, xs_ref, g2_ref, wga_ref, wgb_ref, wua_ref, wub_ref,
                       wda_ref, wdb_ref, ys_ref):
    del ea_ref, eb_ref, blk_ref
    D = ys_ref.shape[1]

    @pl.when(pl.program_id(0) < nused_ref[0])
    def _():
        x1 = xs_ref[:, :D]
        w_a = xs_ref[:, D:D + 1]
        w_b = xs_ref[:, D + 1:D + 2]
        h = _rms(x1, g2_ref[...]).astype(BF16)

        def ffn(wg_ref, wu_ref, w):
            a = jnp.dot(h, wg_ref[...], preferred_element_type=F32)
            b = jnp.dot(h, wu_ref[...], preferred_element_type=F32)
            return (a * _sigmoid(a) * b * w).astype(BF16)

        y = (jnp.dot(ffn(wga_ref, wua_ref, w_a), wda_ref[...], preferred_element_type=F32)
             + jnp.dot(ffn(wgb_ref, wub_ref, w_b), wdb_ref[...], preferred_element_type=F32))
        ys_ref[...] = x1 + y

    @pl.when(pl.program_id(0) >= nused_ref[0])
    def _():
        ys_ref[...] = jnp.zeros_like(ys_ref)


def _pair_experts(tile_ea, tile_eb, tile_blk, n_used, xs, g2, wg, wu, wd, tm):
    rows, width = xs.shape
    E, D, F = wg.shape
    nt = rows // tm
    by_a = lambda i, ea, eb, blk, nu: (ea[i], 0, 0)
    by_b = lambda i, ea, eb, blk, nu: (eb[i], 0, 0)
    tile = lambda i, ea, eb, blk, nu: (blk[i], 0)
    grid_spec = pltpu.PrefetchScalarGridSpec(
        num_scalar_prefetch=4,
        grid=(nt,),
        in_specs=[pl.BlockSpec((tm, width), tile),
                  pl.BlockSpec((1, D), lambda i, ea, eb, blk, nu: (0, 0)),
                  pl.BlockSpec((None, D, F), by_a), pl.BlockSpec((None, D, F), by_b),
                  pl.BlockSpec((None, D, F), by_a), pl.BlockSpec((None, D, F), by_b),
                  pl.BlockSpec((None, F, D), by_a), pl.BlockSpec((None, F, D), by_b)],
        out_specs=pl.BlockSpec((tm, D), lambda i, ea, eb, blk, nu: (i, 0)),
    )
    return pl.pallas_call(
        _pair_experts_body,
        grid_spec=grid_spec,
        out_shape=jax.ShapeDtypeStruct((rows, D), F32),
        compiler_params=_params(1),
        name="moe_pair_experts",
    )(tile_ea, tile_eb, tile_blk, n_used, xs, g2, wg, wg, wu, wu, wd, wd)


def _rot_half_cols(w):
    half = w.shape[-1] // 2
    return jnp.concatenate([-w[..., half:], w[..., :half]], axis=-1)


def _swap_halves(g):
    half = g.shape[-1] // 2
    return jnp.concatenate([g[..., half:], g[..., :half]], axis=-1)


def kernel(x, positions, norm_attn, w_in, b_gate, norm_ckv, w_ukv, q_norm_mla, k_norm_mla, q_norm_dil, k_norm_dil, w_o_mla, w_o_dil, w_out, norm_ffn, w_router_group, b_router_group, w_router_expert, b_router_expert, w_gate, w_up, w_down):
    B, S, D = x.shape
    T = B * S
    depth = norm_attn.shape[0]
    q_cols = MLA_HEADS * MLA_QK
    c0 = q_cols
    c1 = c0 + MLA_KV_RANK
    c2 = c1 + MLA_ROPE
    c3 = c2 + 3 * N_DIL_GROUPS * DIL_GROUP_COLS

    half = MLA_ROPE // 2
    freqs = ROPE_THETA ** (-jnp.arange(half, dtype=F32) / half)
    lane_freq = jnp.concatenate([jnp.zeros((MLA_NOPE,), F32), freqs, freqs, freqs, freqs])
    lane_phase = np.concatenate([np.full(MLA_NOPE + MLA_ROPE, np.pi / 2, np.float32), np.zeros(MLA_ROPE, np.float32)])
    cs = jnp.sin(positions.astype(F32).reshape(T, 1) * lane_freq + jnp.asarray(lane_phase))

    ex_np = np.zeros((LANES, DIL_GROUP_COLS), np.float32)
    for p in range(DIL_HEADS // 2):
        ex_np[DIL_HEAD_DIM + p, (2 * p) * DIL_HEAD_DIM:(2 * p + 1) * DIL_HEAD_DIM] = 1.0
        ex_np[p, (2 * p + 1) * DIL_HEAD_DIM:(2 * p + 2) * DIL_HEAD_DIM] = 1.0
    ex = jnp.asarray(np.concatenate([ex_np, ex_np], axis=0), dtype=BF16)
    bd = jnp.asarray(np.kron(np.eye(2, dtype=np.float32),
                             np.full((DIL_HEAD_DIM, DIL_HEAD_DIM), 1.0 / DIL_HEAD_DIM, np.float32)),
                     dtype=BF16)

    xcur = x.reshape(T, D)
    for l in range(depth):
        w = w_in[l]
        wq = w[:, :c0].reshape(D, MLA_HEADS, MLA_QK)
        wq = jnp.concatenate([wq, _rot_half_cols(wq[:, :, MLA_NOPE:])], axis=-1).reshape(D, MLA_HEADS * LANES)
        wkr = w[:, c1:c2]
        wc = jnp.concatenate([w[:, c0:c1], jnp.zeros((D, MLA_NOPE), F32), wkr, _rot_half_cols(wkr)], axis=-1)
        wd = w[:, c2:c3].reshape(D, 3, N_DIL_GROUPS, DIL_GROUP_COLS).transpose(0, 2, 1, 3).reshape(D, c3 - c2)
        wgate = w[:, c3:]

        wukv = w_ukv[l].reshape(MLA_KV_RANK, MLA_HEADS, MLA_NOPE + MLA_V)
        wk = jnp.concatenate([wukv[:, :, :MLA_NOPE], jnp.zeros((MLA_KV_RANK, MLA_HEADS, LANES - MLA_NOPE), F32)], axis=-1)
        wukv2 = jnp.concatenate([wk.reshape(MLA_KV_RANK, MLA_HEADS * LANES),
                                 wukv[:, :, MLA_NOPE:].reshape(MLA_KV_RANK, MLA_HEADS * MLA_V)], axis=-1)

        qn, kn = q_norm_mla[l], k_norm_mla[l]
        scale = MLA_QK ** -0.5 * LOG2E
        gq = (jnp.concatenate([qn, _swap_halves(qn[MLA_NOPE:])]) * scale).reshape(1, LANES)
        gkn = jnp.concatenate([kn[:MLA_NOPE], jnp.zeros((LANES - MLA_NOPE,), F32)]).reshape(1, LANES)
        gkr = jnp.concatenate([jnp.zeros((MLA_NOPE,), F32), kn[MLA_NOPE:], _swap_halves(kn[MLA_NOPE:])]).reshape(1, LANES)

        q1, cc, d0, d12 = _inproj(xcur, norm_attn[l].reshape(1, D), wq.astype(BF16), wc.astype(BF16),
                                  wd.astype(BF16), tm=512)
        qf, kf, v = _mla_prep(q1, cc, cs, gq, gkn, gkr, norm_ckv[l].reshape(1, MLA_KV_RANK),
                              wukv2.astype(BF16), tm=512)
        o_mla = _mla_attn(qf, kf, v, B, S, tq=512)

        ods, ms, ls = [], [], []
        for g in range(N_DIL_GROUPS):
            gq2 = (jnp.tile(q_norm_dil[l, g], 2) * (DIL_HEAD_DIM ** -0.5 * LOG2E)).reshape(1, LANES)
            gk2 = jnp.tile(k_norm_dil[l, g], 2).reshape(1, LANES)
            if g == 0:
                o, m, den = _dil_attn(d0, 0, gq2, gk2, bd, B, S, g, nsub=1)
            else:
                o, m, den = _dil_attn(d12, (g - 1) * 3, gq2, gk2, bd, B, S, g, nsub=1 if g == 1 else 4)
            ods.append(o)
            ms.append(m)
            ls.append(den)

        pad_cols = lambda a: jnp.pad(a, ((0, 0), (0, LANES - a.shape[1])))
        wr = pad_cols(jnp.concatenate([w_router_expert[l], w_router_group[l]], axis=1))
        br = pad_cols(jnp.concatenate([b_router_expert[l], b_router_group[l]]).reshape(1, -1))
        wrh = wr.astype(BF16)
        wrl = jnp.concatenate([wrh, (wr - wrh.astype(F32)).astype(BF16)], axis=1)
        x1w, onehot, cnt = _outproj(
            xcur, norm_attn[l].reshape(1, D), wgate.astype(BF16), b_gate[l].reshape(1, 2 * D),
            o_mla, ods, ms, ls, ex, w_o_mla[l].astype(BF16), w_o_dil[l].astype(BF16), w_out[l].astype(BF16),
            norm_ffn[l].reshape(1, D), wrh, wrl, br, tm=512)

        counts = cnt[0].astype(jnp.int32)
        tiles_per = (counts + MOE_TILE - 1) // MOE_TILE
        cum = jnp.cumsum(tiles_per)
        offs = ((cum - tiles_per) * MOE_TILE).astype(F32).reshape(1, LANES)
        n_tiles = T // MOE_TILE + N_CLASSES
        blk = jnp.minimum(jnp.arange(n_tiles, dtype=jnp.int32), cum[-1] - 1)
        tcls = jnp.minimum(jnp.sum((cum[None, :] <= blk[:, None]).astype(jnp.int32), axis=1), N_CLASSES - 1)
        tile_ea = jnp.asarray(_CLASS_EXPERT_A)[tcls]
        tile_eb = jnp.asarray(_CLASS_EXPERT_B)[tcls]

        pos = _positions(onehot, offs, chunk=512)
        xs = _permute_rows(pos, x1w, n_tiles * MOE_TILE, gather=False)
        ys = _pair_experts(tile_ea, tile_eb, blk, cum[-1:].astype(jnp.int32), xs, norm_ffn[l].reshape(1, D),
                           w_gate[l].astype(BF16), w_up[l].astype(BF16), w_down[l].astype(BF16), MOE_TILE)
        xcur = _permute_rows(pos, ys, T, gather=True)
    return xcur.reshape(B, S, D)
```
